```python
import math
import jax, jax.numpy as jnp
from jax import lax
import numpy as np

D_MODEL = 1024
BATCH = 4
SEQ = 8192
DEPTH = 2

N_A_LAYERS = DEPTH // 2
N_B_LAYERS = DEPTH - N_A_LAYERS
D_FF = 2816
FFN_HALF = 0.5
SSM_EXPAND = 2
SSM_D_INNER = SSM_EXPAND * D_MODEL
SSM_HEAD_DIM = 64
SSM_HEADS = SSM_D_INNER // SSM_HEAD_DIM
SSM_GROUPS = 4
SSM_STATE = 128
SSM_CONV = 4
SSM_CHUNK = 256
SSM_CONV_DIM = SSM_D_INNER + 2 * SSM_GROUPS * SSM_STATE
SSM_IN_DIM = SSM_D_INNER + SSM_CONV_DIM + SSM_HEADS
ATT_HEAD_DIM = 64
ATT_HEADS = D_MODEL // ATT_HEAD_DIM
ATT_KV_HEADS = ATT_HEADS // 8
ATT_GROUP = ATT_HEADS // ATT_KV_HEADS
ATT_WINDOW = 128
ATT_BLOCK = ATT_WINDOW
REL_BUCKETS = 32
REL_MAX_DIST = ATT_WINDOW
EPS = 1e-6

kernel_name = 'yoco_mamba2_swa_sink_macaron'


def rmsnorm(x, g):
    xf = x.astype(jnp.float32)
    y = xf * lax.rsqrt(jnp.mean(xf * xf, axis=-1, keepdims=True) + EPS)
    return (y * g.astype(jnp.float32)).astype(x.dtype)


def swiglu_half_step(h, g, w1, w3, w2):
    u = rmsnorm(h, g)
    return h + FFN_HALF * ((jax.nn.silu(u @ w1) * (u @ w3)) @ w2)


def causal_depthwise_conv(x, w, b):
    c = x.shape[-1]
    y = lax.conv_general_dilated(
        x, w.astype(x.dtype)[:, None, :], window_strides=(1,),
        padding=[(SSM_CONV - 1, 0)], dimension_numbers=('NWC', 'WIO', 'NWC'),
        feature_group_count=c)
    return y + b.astype(x.dtype)


def ssd_chunked_scan(x, dt, a, b_in, c_in):
    bsz, t, h, p = x.shape
    g, n = b_in.shape[2], b_in.shape[3]
    r = h // g
    L = SSM_CHUNK
    nc = -(-t // L)
    pad = nc * L - t
    f32 = jnp.float32

    def chunks(z):
        z = z.astype(f32)
        z = jnp.pad(z, [(0, 0), (0, pad)] + [(0, 0)] * (z.ndim - 2))
        z = z.reshape((bsz, nc, L) + z.shape[2:])
        return jnp.moveaxis(z, 1, 0)

    xc = chunks(x.reshape(bsz, t, g, r, p))
    dtc = chunks(dt.reshape(bsz, t, g, r))
    bc = chunks(b_in)
    cc = chunks(c_in)
    a_gr = a.astype(f32).reshape(g, r)
    causal = jnp.tril(jnp.ones((L, L), bool))[:, :, None, None]

    def step(state, inp):
        xk, dtk, bk, ck = inp
        acum = jnp.cumsum(dtk * a_gr, axis=1)
        seg = acum[:, :, None] - acum[:, None, :]
        decay = jnp.exp(jnp.where(causal, seg, -jnp.inf))
        cb = jnp.einsum('blgn,bsgn->blsg', ck, bk)
        scores = cb[..., None] * decay
        y_diag = jnp.einsum('blsgr,bsgr,bsgrp->blgrp', scores, dtk, xk)
        y_off = jnp.einsum('blgn,bgrpn,blgr->blgrp', ck, state, jnp.exp(acum))
        w_end = jnp.exp(acum[:, -1:] - acum) * dtk
        state = (state * jnp.exp(acum[:, -1])[..., None, None]
                 + jnp.einsum('bsgn,bsgr,bsgrp->bgrpn', bk, w_end, xk))
        return state, y_diag + y_off

    state0 = jnp.zeros((bsz, g, r, p, n), f32)
    _, y = lax.scan(step, state0, (xc, dtc, bc, cc))
    y = jnp.moveaxis(y, 0, 1).reshape(bsz, nc * L, h, p)[:, :t]
    return y.astype(x.dtype)


def mamba2_mixer(u, w_in, conv_w, conv_b, dt_bias, a_log, d_skip, gate_norm, w_out):
    bsz, t, _ = u.shape
    zxbcdt = u @ w_in
    z, xbc, dt = jnp.split(zxbcdt, [SSM_D_INNER, SSM_D_INNER + SSM_CONV_DIM], axis=-1)
    xbc = jax.nn.silu(causal_depthwise_conv(xbc, conv_w, conv_b))
    xs, b_in, c_in = jnp.split(xbc, [SSM_D_INNER, SSM_D_INNER + SSM_GROUPS * SSM_STATE], axis=-1)
    xs = xs.reshape(bsz, t, SSM_HEADS, SSM_HEAD_DIM)
    b_in = b_in.reshape(bsz, t, SSM_GROUPS, SSM_STATE)
    c_in = c_in.reshape(bsz, t, SSM_GROUPS, SSM_STATE)
    dt = jax.nn.softplus((dt + dt_bias).astype(jnp.float32))
    a = -jnp.exp(a_log.astype(jnp.float32))
    y = ssd_chunked_scan(xs, dt, a, b_in, c_in) + d_skip[:, None].astype(xs.dtype) * xs
    y = y.reshape(bsz, t, SSM_D_INNER) * jax.nn.silu(z)
    y = rmsnorm(y.reshape(bsz, t, SSM_GROUPS, SSM_D_INNER // SSM_GROUPS),
                gate_norm.reshape(SSM_GROUPS, SSM_D_INNER // SSM_GROUPS))
    return y.reshape(bsz, t, SSM_D_INNER) @ w_out


def shared_kv(h, kv_norm, w_kv, k_norm):
    bsz, t, _ = h.shape
    kv = rmsnorm(h, kv_norm) @ w_kv
    k, v = jnp.split(kv, 2, axis=-1)
    k = rmsnorm(k.reshape(bsz, t, ATT_KV_HEADS, ATT_HEAD_DIM), k_norm)
    v = v.reshape(bsz, t, ATT_KV_HEADS, ATT_HEAD_DIM)
    return k, v


def t5_bucket(dist):
    n = jnp.maximum(dist, 0)
    max_exact = REL_BUCKETS // 2
    nf = jnp.maximum(n, 1).astype(jnp.float32)
    large = max_exact + (jnp.log(nf / max_exact) / math.log(REL_MAX_DIST / max_exact)
                         * (REL_BUCKETS - max_exact)).astype(jnp.int32)
    large = jnp.minimum(large, REL_BUCKETS - 1)
    return jnp.where(n < max_exact, n, large)


def sliding_window_attention(u, k, v, w_q, q_norm, sinks, rel_bias, w_o):
    bsz, t, _ = u.shape
    blk = ATT_BLOCK
    nb = t // blk
    q = rmsnorm((u @ w_q).reshape(bsz, t, ATT_KV_HEADS, ATT_GROUP, ATT_HEAD_DIM), q_norm)
    qb = jnp.moveaxis(q.reshape(bsz, nb, blk, ATT_KV_HEADS, ATT_GROUP, ATT_HEAD_DIM), 1, 0)

    def band(z):
        prev = jnp.pad(z, [(0, 0), (blk, 0), (0, 0), (0, 0)])[:, :t]
        zz = jnp.concatenate([prev.reshape(bsz, nb, blk, ATT_KV_HEADS, ATT_HEAD_DIM),
                              z.reshape(bsz, nb, blk, ATT_KV_HEADS, ATT_HEAD_DIM)], axis=2)
        return jnp.moveaxis(zz, 1, 0)

    kb, vb = band(k), band(v)
    qi = jnp.arange(blk)[:, None] + blk
    kj = jnp.arange(2 * blk)[None, :]
    dist = qi - kj
    in_window = (dist >= 0) & (dist < ATT_WINDOW)
    bias = rel_bias[t5_bucket(dist)]
    bias = jnp.transpose(bias.reshape(blk, 2 * blk, ATT_KV_HEADS, ATT_GROUP),
                         (2, 3, 0, 1)).astype(jnp.float32)
    sink = sinks.reshape(ATT_KV_HEADS, ATT_GROUP)[None, :, :, None, None].astype(jnp.float32)
    scale = ATT_HEAD_DIM ** -0.5

    def block(args):
        qk, kk, vk, bi = args
        s = jnp.einsum('bqkrd,bskd->bkrqs', qk, kk).astype(jnp.float32) * scale + bias
        valid = in_window & ((bi > 0) | (kj >= blk))
        s = jnp.where(valid, s, -jnp.inf)
        m = jnp.maximum(jnp.max(s, axis=-1, keepdims=True), sink)
        p = jnp.exp(s - m)
        denom = jnp.sum(p, axis=-1, keepdims=True) + jnp.exp(sink - m)
        return jnp.einsum('bkrqs,bskd->bqkrd', (p / denom).astype(vk.dtype), vk)

    o = lax.map(block, (qb, kb, vb, jnp.arange(nb)))
    o = jnp.moveaxis(o, 0, 1).reshape(bsz, t, ATT_HEADS * ATT_HEAD_DIM)
    return o @ w_o


def setup_inputs(seed: int = 0) -> dict:
    key = jax.random.key(seed)
    ks = jax.random.split(key, 24)
    nrm = jax.random.normal
    f32 = jnp.float32
    x = nrm(ks[0], (BATCH, SEQ, D_MODEL), f32)
    ffn_norm = 1.0 + 0.05 * nrm(ks[1], (DEPTH, 2, D_MODEL), f32)
    ffn_w1 = nrm(ks[2], (DEPTH, 2, D_MODEL, D_FF), f32) * D_MODEL ** -0.5
    ffn_w3 = nrm(ks[3], (DEPTH, 2, D_MODEL, D_FF), f32) * D_MODEL ** -0.5
    ffn_w2 = nrm(ks[4], (DEPTH, 2, D_FF, D_MODEL), f32) * D_FF ** -0.5
    ssm_norm = 1.0 + 0.05 * nrm(ks[5], (N_A_LAYERS, D_MODEL), f32)
    ssm_w_in = nrm(ks[6], (N_A_LAYERS, D_MODEL, SSM_IN_DIM), f32) * D_MODEL ** -0.5
    ssm_conv_w = nrm(ks[7], (N_A_LAYERS, SSM_CONV, SSM_CONV_DIM), f32) * SSM_CONV ** -0.5
    ssm_conv_b = 0.01 * nrm(ks[8], (N_A_LAYERS, SSM_CONV_DIM), f32)
    dt0 = jnp.exp(jax.random.uniform(ks[9], (N_A_LAYERS, SSM_HEADS), f32,
                                     math.log(1e-3), math.log(1e-1)))
    ssm_dt_bias = dt0 + jnp.log(-jnp.expm1(-dt0))
    ssm_a_log = jnp.log(jax.random.uniform(ks[10], (N_A_LAYERS, SSM_HEADS), f32, 1.0, 16.0))
    ssm_d = 1.0 + 0.1 * nrm(ks[11], (N_A_LAYERS, SSM_HEADS), f32)
    ssm_gate_norm = 1.0 + 0.05 * nrm(ks[12], (N_A_LAYERS, SSM_D_INNER), f32)
    ssm_w_out = nrm(ks[13], (N_A_LAYERS, SSM_D_INNER, D_MODEL), f32) * SSM_D_INNER ** -0.5
    kv_norm = 1.0 + 0.05 * nrm(ks[14], (D_MODEL,), f32)
    w_kv = nrm(ks[15], (D_MODEL, 2 * ATT_KV_HEADS * ATT_HEAD_DIM), f32) * D_MODEL ** -0.5
    k_norm = 1.0 + 0.05 * nrm(ks[16], (ATT_HEAD_DIM,), f32)
    attn_norm = 1.0 + 0.05 * nrm(ks[17], (N_B_LAYERS, D_MODEL), f32)
    w_q = nrm(ks[18], (N_B_LAYERS, D_MODEL, ATT_HEADS * ATT_HEAD_DIM), f32) * D_MODEL ** -0.5
    q_norm = 1.0 + 0.05 * nrm(ks[19], (N_B_LAYERS, ATT_HEAD_DIM), f32)
    sinks = 0.5 * nrm(ks[20], (N_B_LAYERS, ATT_HEADS), f32)
    w_o = nrm(ks[21], (N_B_LAYERS, ATT_HEADS * ATT_HEAD_DIM, D_MODEL), f32) * (ATT_HEADS * ATT_HEAD_DIM) ** -0.5
    rel_bias = 0.5 * nrm(ks[22], (REL_BUCKETS, ATT_HEADS), f32)
    return {'x': x, 'ffn_norm': ffn_norm, 'ffn_w1': ffn_w1, 'ffn_w3': ffn_w3, 'ffn_w2': ffn_w2,
            'ssm_norm': ssm_norm, 'ssm_w_in': ssm_w_in, 'ssm_conv_w': ssm_conv_w,
            'ssm_conv_b': ssm_conv_b, 'ssm_dt_bias': ssm_dt_bias, 'ssm_a_log': ssm_a_log,
            'ssm_d': ssm_d, 'ssm_gate_norm': ssm_gate_norm, 'ssm_w_out': ssm_w_out,
            'kv_norm': kv_norm, 'w_kv': w_kv, 'k_norm': k_norm,
            'attn_norm': attn_norm, 'w_q': w_q, 'q_norm': q_norm, 'sinks': sinks, 'w_o': w_o,
            'rel_bias': rel_bias}


def reference(x, ffn_norm, ffn_w1, ffn_w3, ffn_w2,
              ssm_norm, ssm_w_in, ssm_conv_w, ssm_conv_b, ssm_dt_bias, ssm_a_log,
              ssm_d, ssm_gate_norm, ssm_w_out,
              kv_norm, w_kv, k_norm,
              attn_norm, w_q, q_norm, sinks, w_o,
              rel_bias):
    h = x
    k_shared, v_shared = None, None
    for layer in range(DEPTH):
        h = swiglu_half_step(h, ffn_norm[layer, 0], ffn_w1[layer, 0], ffn_w3[layer, 0], ffn_w2[layer, 0])
        if layer < N_A_LAYERS:
            i = layer
            h = h + mamba2_mixer(rmsnorm(h, ssm_norm[i]), ssm_w_in[i], ssm_conv_w[i], ssm_conv_b[i],
                                 ssm_dt_bias[i], ssm_a_log[i], ssm_d[i], ssm_gate_norm[i], ssm_w_out[i])
        else:
            j = layer - N_A_LAYERS
            h = h + sliding_window_attention(rmsnorm(h, attn_norm[j]), k_shared, v_shared,
                                             w_q[j], q_norm[j], sinks[j], rel_bias, w_o[j])
        h = swiglu_half_step(h, ffn_norm[layer, 1], ffn_w1[layer, 1], ffn_w3[layer, 1], ffn_w2[layer, 1])
        if layer == N_A_LAYERS - 1:
            k_shared, v_shared = shared_kv(h, kv_norm, w_kv, k_norm)
    return h
```

```python
import functools
import math

import numpy as np
import jax
import jax.numpy as jnp
from jax import lax
from jax.experimental import pallas as pl
from jax.experimental.pallas import tpu as pltpu

F32 = jnp.float32
BF16 = jnp.bfloat16

EPS = 1e-6
FFN_HALF = 0.5

SSM_HEAD_DIM = 64
SSM_GROUPS = 4
SSM_STATE = 128
SSM_CONV = 4
SSM_CHUNK = 256
ATT_HEAD_DIM = 64
ATT_KV_HEADS = 2
ATT_WINDOW = 128
REL_BUCKETS = 32

V7X_LANES = 128
V7X_SUBLANES = 8
V7X_VMEM_LIMIT_BYTES = 56 * 1024 * 1024

FFN_TOKEN_TILE = 512
FFN_CHUNK = 256
KV_TOKEN_TILE = 1024


def _rms_scale(x, g):
    ms = jnp.mean(x * x, axis=-1, keepdims=True)
    return x * lax.rsqrt(ms + EPS) * g


def _sigmoid(x):
    return 0.5 * jnp.tanh(0.5 * x) + 0.5


def _silu(x):
    return x * _sigmoid(x)


def _const_spec(shape):
    zeros = (0,) * len(shape)
    return pl.BlockSpec(shape, lambda *_: zeros, pipeline_mode=pl.Buffered(1))


def _ffn_kernel(h_ref, g_ref, w1_ref, w3_ref, w2_ref, o_ref, acc_ref, *, n_chunks):
    h = h_ref[...]
    u = _rms_scale(h, g_ref[...]).astype(BF16)
    for c in range(n_chunks):
        a = jnp.dot(u, w1_ref[c], preferred_element_type=F32)
        b = jnp.dot(u, w3_ref[c], preferred_element_type=F32)
        gate = (_silu(a) * b).astype(BF16)
        d = jnp.dot(gate, w2_ref[c], preferred_element_type=F32)
        if c == 0:
            acc_ref[...] = d
        else:
            acc_ref[...] += d
    o_ref[...] = h + FFN_HALF * acc_ref[...]


def _ffn_half_step(h, g, w1, w3, w2):
    n, d = h.shape
    f = w1.shape[1]
    tm = min(FFN_TOKEN_TILE, n)
    fc = FFN_CHUNK
    assert n % tm == 0 and f % fc == 0
    nc = f // fc
    w1c = w1.astype(BF16).reshape(d, nc, fc).transpose(1, 0, 2)
    w3c = w3.astype(BF16).reshape(d, nc, fc).transpose(1, 0, 2)
    w2c = w2.astype(BF16).reshape(nc, fc, d)
    return pl.pallas_call(
        functools.partial(_ffn_kernel, n_chunks=nc),
        out_shape=jax.ShapeDtypeStruct((n, d), F32),
        grid=(n // tm,),
        in_specs=[
            pl.BlockSpec((tm, d), lambda i: (i, 0)),
            _const_spec((1, d)),
            _const_spec((nc, d, fc)),
            _const_spec((nc, d, fc)),
            _const_spec((nc, fc, d)),
        ],
        out_specs=pl.BlockSpec((tm, d), lambda i: (i, 0)),
        scratch_shapes=[pltpu.VMEM((tm, d), F32)],
        compiler_params=pltpu.CompilerParams(
            dimension_semantics=("arbitrary",), vmem_limit_bytes=V7X_VMEM_LIMIT_BYTES),
        name="ffn_half_step",
    )(h, g.reshape(1, d), w1c, w3c, w2c)


def _pair_cols(v, r0, lane_lo):
    return jnp.where(lane_lo, v[:, r0:r0 + 1], v[:, r0 + 1:r0 + 2])


def _mamba_kernel(h_ref, g_ref, wz_ref, wx_ref, wdt_ref, wdtT_ref, cw_ref, cb_ref,
                  dtb_ref, dtbT_ref, alog_ref, alogT_ref, dskip_ref, gn_ref, wo_ref,
                  o_ref, xpad_ref, state_ref, y_ref, *, n_heads):
    L = h_ref.shape[0]
    d_inner = wz_ref.shape[1]
    gs = SSM_GROUPS * SSM_STATE
    hp = SSM_HEAD_DIM
    heads_per_group = n_heads // SSM_GROUPS
    pairs_per_group = heads_per_group // 2
    pad = V7X_SUBLANES

    @pl.when(pl.program_id(1) == 0)
    def _():
        state_ref[...] = jnp.zeros_like(state_ref)
        xpad_ref[0:pad, :] = jnp.zeros((pad, xpad_ref.shape[1]), F32)

    h = h_ref[...]
    u = _rms_scale(h, g_ref[...]).astype(BF16)

    z = jnp.dot(u, wz_ref[...], preferred_element_type=F32)
    xpad_ref[pad:pad + L, :] = jnp.dot(u, wx_ref[...], preferred_element_type=F32)
    dt_raw = jnp.dot(u, wdt_ref[...], preferred_element_type=F32)[:, :n_heads]
    dt_rawT = lax.dot_general(wdtT_ref[...], u, (((1,), (1,)), ((), ())),
                              preferred_element_type=F32)

    conv = cb_ref[...]
    for k in range(SSM_CONV):
        off = pad - (SSM_CONV - 1) + k
        conv = conv + cw_ref[k:k + 1, :] * xpad_ref[off:off + L, :]
    xpad_ref[0:pad, :] = xpad_ref[L:L + pad, :]
    xbc = _silu(conv)
    xs = xbc[:, :d_inner]
    b_all = xbc[:, d_inner:d_inner + gs].astype(BF16)
    c_all = xbc[:, d_inner + gs:].astype(BF16)

    def softplus(x):
        return jnp.maximum(x, 0.0) + jnp.log1p(jnp.exp(-jnp.abs(x)))

    dt = softplus(dt_raw + dtb_ref[...])
    dtT = softplus(dt_rawT + dtbT_ref[...])
    a_row = -jnp.exp(alog_ref[...])
    a_col = -jnp.exp(alogT_ref[...])

    ri = lax.broadcasted_iota(jnp.int32, (L, L), 0)
    ci = lax.broadcasted_iota(jnp.int32, (L, L), 1)
    causal = ri >= ci
    tri = jnp.where(causal, 1.0, 0.0).astype(F32)
    triT = jnp.where(ci >= ri, 1.0, 0.0).astype(F32)
    acum = jnp.dot(tri, dt * a_row, precision=lax.Precision.HIGHEST,
                   preferred_element_type=F32)
    acumT = jnp.dot(dtT * a_col, triT, precision=lax.Precision.HIGHEST,
                    preferred_element_type=F32)
    a_last = acum[L - 1:L, :]
    e_acum = jnp.exp(acum)
    e_last = jnp.exp(a_last)
    w_end = jnp.exp(a_last - acum) * dt

    lane_lo = lax.broadcasted_iota(jnp.int32, (1, 2 * hp), 1) < hp

    for g in range(SSM_GROUPS):
        cg = c_all[:, g * SSM_STATE:(g + 1) * SSM_STATE]
        bg = b_all[:, g * SSM_STATE:(g + 1) * SSM_STATE]
        cb = lax.dot_general(cg, bg, (((1,), (1,)), ((), ())),
                             preferred_element_type=F32)
        st = state_ref[g]
        y_off = jnp.dot(cg, st.astype(BF16), preferred_element_type=F32)
        xw_parts = []
        el_parts = []
        for j in range(pairs_per_group):
            r0 = g * heads_per_group + 2 * j
            col0 = r0 * hp
            xs_pair = xs[:, col0:col0 + 2 * hp]
            xd_pair = (xs_pair * _pair_cols(dt, r0, lane_lo)).astype(BF16)
            yd = []
            for r in (r0, r0 + 1):
                seg = acum[:, r:r + 1] - acumT[r:r + 1, :]
                dec = jnp.exp(jnp.where(causal, seg, -jnp.inf))
                m = (cb * dec).astype(BF16)
                yd.append(jnp.dot(m, xd_pair, preferred_element_type=F32))
            y_diag = jnp.where(lane_lo, yd[0], yd[1])
            y_pair = y_diag + y_off[:, 2 * j * hp:(2 * j + 2) * hp] * _pair_cols(e_acum, r0, lane_lo)
            y_ref[:, col0:col0 + 2 * hp] = y_pair
            xw_parts.append((xs_pair * _pair_cols(w_end, r0, lane_lo)).astype(BF16))
            el_parts.append(_pair_cols(e_last, r0, lane_lo))
        xw = jnp.concatenate(xw_parts, axis=1)
        el = jnp.concatenate(el_parts, axis=1)
        inc = lax.dot_general(bg, xw, (((0,), (0,)), ((), ())),
                              preferred_element_type=F32)
        state_ref[g] = st * el + inc

    y = (y_ref[...] + dskip_ref[...] * xs) * _silu(z)
    gw = d_inner // SSM_GROUPS
    parts = []
    for g in range(SSM_GROUPS):
        yg = y[:, g * gw:(g + 1) * gw]
        parts.append(_rms_scale(yg, gn_ref[:, g * gw:(g + 1) * gw]).astype(BF16))
    yn = jnp.concatenate(parts, axis=1)
    o_ref[...] = h + jnp.dot(yn, wo_ref[...], preferred_element_type=F32)


def _mamba_block(h3, norm_g, w_in, conv_w, conv_b, dt_bias, a_log, d_skip, gate_norm, w_out):
    bsz, t, d = h3.shape
    n_heads = dt_bias.shape[0]
    d_inner = n_heads * SSM_HEAD_DIM
    conv_dim = d_inner + 2 * SSM_GROUPS * SSM_STATE
    assert w_in.shape == (d, d_inner + conv_dim + n_heads)
    assert n_heads % (2 * SSM_GROUPS) == 0 and n_heads <= V7X_LANES
    L = SSM_CHUNK
    assert t % L == 0
    hpg = n_heads // SSM_GROUPS

    w_in16 = w_in.astype(BF16)
    wz = w_in16[:, :d_inner]
    wx = w_in16[:, d_inner:d_inner + conv_dim]
    wdt = w_in16[:, d_inner + conv_dim:]
    wdt_pad = jnp.pad(wdt, ((0, 0), (0, V7X_LANES - n_heads)))
    wdtT = wdt.T
    dskip_row = jnp.repeat(d_skip.astype(F32), SSM_HEAD_DIM).reshape(1, d_inner)

    operands = [
        (norm_g.reshape(1, d), None),
        (wz, None), (wx, None), (wdt_pad, None), (wdtT, None),
        (conv_w.astype(F32), None), (conv_b.reshape(1, conv_dim).astype(F32), None),
        (dt_bias.reshape(1, n_heads).astype(F32), None),
        (dt_bias.reshape(n_heads, 1).astype(F32), None),
        (a_log.reshape(1, n_heads).astype(F32), None),
        (a_log.reshape(n_heads, 1).astype(F32), None),
        (dskip_row, None),
        (gate_norm.reshape(1, d_inner).astype(F32), None),
        (w_out.astype(BF16), None),
    ]
    arrays = [a for a, _ in operands]
    in_specs = [pl.BlockSpec((None, L, d), lambda b, c: (b, c, 0))]
    in_specs += [_const_spec(a.shape) for a in arrays]
    return pl.pallas_call(
        functools.partial(_mamba_kernel, n_heads=n_heads),
        out_shape=jax.ShapeDtypeStruct((bsz, t, d), F32),
        grid=(bsz, t // L),
        in_specs=in_specs,
        out_specs=pl.BlockSpec((None, L, d), lambda b, c: (b, c, 0)),
        scratch_shapes=[
            pltpu.VMEM((L + V7X_SUBLANES, conv_dim), F32),
            pltpu.VMEM((SSM_GROUPS, SSM_STATE, hpg * SSM_HEAD_DIM), F32),
            pltpu.VMEM((L, d_inner), F32),
        ],
        compiler_params=pltpu.CompilerParams(
            dimension_semantics=("arbitrary", "arbitrary"),
            vmem_limit_bytes=V7X_VMEM_LIMIT_BYTES),
        name="mamba2_block",
    )(h3, *arrays)


def _kv_kernel(h_ref, g_ref, w_ref, kn_ref, k_ref, v_ref):
    kw = k_ref.shape[1]
    u = _rms_scale(h_ref[...], g_ref[...]).astype(BF16)
    kv = jnp.dot(u, w_ref[...], preferred_element_type=F32)
    k = kv[:, :kw]
    ri = lax.broadcasted_iota(jnp.int32, (kw, kw), 0) // ATT_HEAD_DIM
    ci = lax.broadcasted_iota(jnp.int32, (kw, kw), 1) // ATT_HEAD_DIM
    seg = jnp.where(ri == ci, 1.0, 0.0).astype(F32)
    ssq = jnp.dot(k * k, seg, precision=lax.Precision.HIGHEST, preferred_element_type=F32)
    k = k * lax.rsqrt(ssq * (1.0 / ATT_HEAD_DIM) + EPS) * kn_ref[...]
    k_ref[...] = k.astype(k_ref.dtype)
    v_ref[...] = kv[:, kw:].astype(v_ref.dtype)


def _shared_kv(h, kv_norm, w_kv, k_norm):
    n, d = h.shape
    kw = ATT_KV_HEADS * ATT_HEAD_DIM
    assert w_kv.shape == (d, 2 * kw)
    tm = min(KV_TOKEN_TILE, n)
    assert n % tm == 0
    kn = jnp.tile(k_norm.astype(F32), ATT_KV_HEADS).reshape(1, kw)
    return pl.pallas_call(
        _kv_kernel,
        out_shape=(jax.ShapeDtypeStruct((n, kw), BF16), jax.ShapeDtypeStruct((n, kw), BF16)),
        grid=(n // tm,),
        in_specs=[
            pl.BlockSpec((tm, d), lambda i: (i, 0)),
            _const_spec((1, d)),
            _const_spec((d, 2 * kw)),
            _const_spec((1, kw)),
        ],
        out_specs=(pl.BlockSpec((tm, kw), lambda i: (i, 0)), pl.BlockSpec((tm, kw), lambda i: (i, 0))),
        compiler_params=pltpu.CompilerParams(
            dimension_semantics=("arbitrary",), vmem_limit_bytes=V7X_VMEM_LIMIT_BYTES),
        name="shared_kv",
    )(h, kv_norm.reshape(1, d), w_kv.astype(BF16), kn)


def _t5_bucket_table(blk):
    qi = jnp.arange(blk)[:, None] + blk
    kj = jnp.arange(2 * blk)[None, :]
    dist = qi - kj
    n = jnp.maximum(dist, 0)
    max_exact = REL_BUCKETS // 2
    nf = jnp.maximum(n, 1).astype(F32)
    large = max_exact + (jnp.log(nf / max_exact) / math.log(ATT_WINDOW / max_exact)
                         * (REL_BUCKETS - max_exact)).astype(jnp.int32)
    large = jnp.minimum(large, REL_BUCKETS - 1)
    bucket = jnp.where(n < max_exact, n, large)
    in_window = (dist >= 0) & (dist < ATT_WINDOW)
    return jnp.where(in_window, bucket, -1).astype(jnp.int32)


def _attn_kernel(relb_ref, sink_ref, h_ref, g_ref, wq_ref, qn_ref, kp_ref, kc_ref, vp_ref, vc_ref,
                 bucket_ref, wo_ref, o_ref, bias_ref, *, n_heads):
    tq = h_ref.shape[0]
    hd = ATT_HEAD_DIM
    group = n_heads // ATT_KV_HEADS
    pairs = group // 2
    first = (pl.program_id(0) == 0) & (pl.program_id(1) == 0)

    @pl.when(first)
    def _():
        bias_ref[...] = jnp.full(bias_ref.shape, -jnp.inf, F32)
        bucket = bucket_ref[...]

        def fill(b, carry):
            hit = bucket == b
            for hh in range(n_heads):
                bias_ref[hh] = jnp.where(hit, relb_ref[b, hh], bias_ref[hh])
            return carry

        lax.fori_loop(0, REL_BUCKETS, fill, 0)

    h = h_ref[...]
    u = _rms_scale(h, g_ref[...]).astype(BF16)
    q = jnp.dot(u, wq_ref[...], preferred_element_type=F32)

    dq = q.shape[1]
    ri = lax.broadcasted_iota(jnp.int32, (dq, V7X_LANES), 0) // hd
    ci = lax.broadcasted_iota(jnp.int32, (dq, V7X_LANES), 1)
    seg = jnp.where(ri == ci, 1.0, 0.0).astype(F32)
    ssq = jnp.dot(q * q, seg, precision=lax.Precision.HIGHEST, preferred_element_type=F32)
    q_scale = lax.rsqrt(ssq * (1.0 / hd) + EPS) * (hd ** -0.5)
    q16 = q.astype(BF16)

    lane = lax.broadcasted_iota(jnp.int32, (1, 2 * hd), 1)
    lane_lo = lane < hd
    kfull = jnp.concatenate([kp_ref[...], kc_ref[...]], axis=0).astype(F32) * qn_ref[...]
    vfull = jnp.concatenate([vp_ref[...], vc_ref[...]], axis=0).astype(F32)
    kswap = pltpu.roll(kfull, hd, 1)
    vswap = pltpu.roll(vfull, hd, 1)

    kj = lax.broadcasted_iota(jnp.int32, (1, 2 * tq), 1)
    key_ok = kj >= jnp.where(pl.program_id(1) > 0, 0, tq)

    out_pairs = []
    for kvh in range(ATT_KV_HEADS):
        if kvh == 0:
            k_even = jnp.where(lane_lo, kfull, 0.0)
            k_odd = jnp.where(lane_lo, 0.0, kswap)
            v_even = jnp.where(lane_lo, vfull, 0.0)
            v_odd = jnp.where(lane_lo, 0.0, vswap)
        else:
            k_even = jnp.where(lane_lo, kswap, 0.0)
            k_odd = jnp.where(lane_lo, 0.0, kfull)
            v_even = jnp.where(lane_lo, vswap, 0.0)
            v_odd = jnp.where(lane_lo, 0.0, vfull)
        k_slots = (k_even.astype(BF16), k_odd.astype(BF16))
        v_stack = jnp.concatenate([v_even, v_odd], axis=0).astype(BF16)
        q_stack = jnp.concatenate(
            [q16[:, (kvh * pairs + j) * 2 * hd:(kvh * pairs + j + 1) * 2 * hd] for j in range(pairs)],
            axis=0)
        s_slots = [lax.dot_general(q_stack, ks, (((1,), (1,)), ((), ())), preferred_element_type=F32)
                   for ks in k_slots]
        for j in range(pairs):
            probs = []
            for e in range(2):
                hh = kvh * group + 2 * j + e
                s = s_slots[e][j * tq:(j + 1) * tq, :] * q_scale[:, hh:hh + 1] + bias_ref[hh]
                s = jnp.where(key_ok, s, -jnp.inf)
                sink = sink_ref[hh]
                m = jnp.maximum(jnp.max(s, axis=-1, keepdims=True), sink)
                p = jnp.exp(s - m)
                denom = jnp.sum(p, axis=-1, keepdims=True) + jnp.exp(sink - m)
                probs.append((p / denom).astype(BF16))
            p_cat = jnp.concatenate(probs, axis=1)
            out_pairs.append(jnp.dot(p_cat, v_stack, preferred_element_type=F32).astype(BF16))
    o = jnp.concatenate(out_pairs, axis=1)
    o_ref[...] = h + jnp.dot(o, wo_ref[...], preferred_element_type=F32)


def _attention_block(h3, k, v, norm_g, w_q, q_norm, sinks, rel_bias, w_o):
    bsz, t, d = h3.shape
    n_heads = sinks.shape[0]
    hd = ATT_HEAD_DIM
    kw = ATT_KV_HEADS * hd
    assert w_q.shape == (d, n_heads * hd) and w_o.shape == (n_heads * hd, d)
    assert n_heads % (2 * ATT_KV_HEADS) == 0 and kw == V7X_LANES
    tq = ATT_WINDOW
    assert t % tq == 0
    bucket = _t5_bucket_table(tq)
    qn = jnp.tile(q_norm.astype(F32), ATT_KV_HEADS).reshape(1, kw)

    def prev_blk(b, i, *_):
        return (b, jnp.maximum(i - 1, 0), 0)

    def cur_blk(b, i, *_):
        return (b, i, 0)

    def const2(b, i, *_):
        return (0, 0)

    one = pl.Buffered(1)
    grid_spec = pltpu.PrefetchScalarGridSpec(
        num_scalar_prefetch=2,
        grid=(bsz, t // tq),
        in_specs=[
            pl.BlockSpec((None, tq, d), cur_blk),
            pl.BlockSpec((1, d), const2, pipeline_mode=one),
            pl.BlockSpec((d, n_heads * hd), const2, pipeline_mode=one),
            pl.BlockSpec((1, kw), const2, pipeline_mode=one),
            pl.BlockSpec((None, tq, kw), prev_blk),
            pl.BlockSpec((None, tq, kw), cur_blk),
            pl.BlockSpec((None, tq, kw), prev_blk),
            pl.BlockSpec((None, tq, kw), cur_blk),
            pl.BlockSpec((tq, 2 * tq), const2, pipeline_mode=one),
            pl.BlockSpec((n_heads * hd, d), const2, pipeline_mode=one),
        ],
        out_specs=pl.BlockSpec((None, tq, d), cur_blk),
        scratch_shapes=[pltpu.VMEM((n_heads, tq, 2 * tq), F32)],
    )
    return pl.pallas_call(
        functools.partial(_attn_kernel, n_heads=n_heads),
        out_shape=jax.ShapeDtypeStruct((bsz, t, d), F32),
        grid_spec=grid_spec,
        compiler_params=pltpu.CompilerParams(
            dimension_semantics=("arbitrary", "arbitrary"),
            vmem_limit_bytes=V7X_VMEM_LIMIT_BYTES),
        name="swa_block",
    )(rel_bias.astype(F32), sinks.astype(F32), h3, norm_g.reshape(1, d), w_q.astype(BF16), qn,
      k, k, v, v, bucket, w_o.astype(BF16))


def kernel(x, ffn_norm, ffn_w1, ffn_w3, ffn_w2, ssm_norm, ssm_w_in, ssm_conv_w, ssm_conv_b, ssm_dt_bias, ssm_a_log, ssm_d, ssm_gate_norm, ssm_w_out, kv_norm, w_kv, k_norm, attn_norm, w_q, q_norm, sinks, w_o, rel_bias):
    bsz, t, d = x.shape
    depth = ffn_norm.shape[0]
    n_a = ssm_norm.shape[0]
    n = bsz * t
    kw = ATT_KV_HEADS * ATT_HEAD_DIM

    def ffn(h3, layer, half):
        out = _ffn_half_step(h3.reshape(n, d), ffn_norm[layer, half], ffn_w1[layer, half],
                             ffn_w3[layer, half], ffn_w2[layer, half])
        return out.reshape(bsz, t, d)

    h = x
    k_shared = v_shared = None
    for layer in range(depth):
        h = ffn(h, layer, 0)
        if layer < n_a:
            i = layer
            h = _mamba_block(h, ssm_norm[i], ssm_w_in[i], ssm_conv_w[i], ssm_conv_b[i], ssm_dt_bias[i],
                             ssm_a_log[i], ssm_d[i], ssm_gate_norm[i], ssm_w_out[i])
        else:
            j = layer - n_a
            h = _attention_block(h, k_shared, v_shared, attn_norm[j], w_q[j], q_norm[j], sinks[j],
                                 rel_bias, w_o[j])
        h = ffn(h, layer, 1)
        if layer == n_a - 1:
            k2, v2 = _shared_kv(h.reshape(n, d), kv_norm, w_kv, k_norm)
            k_shared = k2.reshape(bsz, t, kw)
            v_shared = v2.reshape(bsz, t, kw)
    return h
```

```python
import functools
import math

import jax
import jax.numpy as jnp
from jax import lax
from jax.experimental import pallas as pl
from jax.experimental.pallas import tpu as pltpu

F32 = jnp.float32
BF16 = jnp.bfloat16

EPS = 1e-6
FFN_HALF = 0.5
LOG2E = 1.4426950408889634

SSM_HEAD_DIM = 64
SSM_GROUPS = 4
SSM_STATE = 128
SSM_CONV = 4
ATT_HEAD_DIM = 64
ATT_KV_HEADS = 2
ATT_WINDOW = 128
REL_BUCKETS = 32

V7X_LANES = 128
V7X_SUBLANES = 8
V7X_VMEM_LIMIT_BYTES = 56 * 1024 * 1024

FFN_TOKEN_TILE = 512
FFN_CHUNK = 256
SSM_TOKEN_TILE = 256
SSM_SUBCHUNK = 128
ATT_TOKEN_TILE = 256


def _rms_scale(x, g):
    ms = jnp.mean(x * x, axis=-1, keepdims=True)
    return x * lax.rsqrt(ms + EPS) * g


def _sigmoid(x):
    return 0.5 * jnp.tanh(0.5 * x) + 0.5


def _silu(x):
    return x * _sigmoid(x)


def _const_spec(shape):
    zeros = (0,) * len(shape)
    return pl.BlockSpec(shape, lambda *_: zeros, pipeline_mode=pl.Buffered(1))


def _head_norm_k(k, kn):
    kw = k.shape[1]
    ri = lax.broadcasted_iota(jnp.int32, (kw, kw), 0) // ATT_HEAD_DIM
    ci = lax.broadcasted_iota(jnp.int32, (kw, kw), 1) // ATT_HEAD_DIM
    seg = jnp.where(ri == ci, 1.0, 0.0).astype(F32)
    ssq = jnp.dot(k * k, seg, precision=lax.Precision.HIGHEST, preferred_element_type=F32)
    return k * lax.rsqrt(ssq * (1.0 / ATT_HEAD_DIM) + EPS) * kn


def _ffn_kernel(*refs, n_chunks, fc, with_kv):
    if with_kv:
        h_ref, g_ref, w1_ref, w3_ref, w2_ref, kvg_ref, wkv_ref, kn_ref, o_ref, k_ref, v_ref, acc_ref = refs
    else:
        h_ref, g_ref, w1_ref, w3_ref, w2_ref, o_ref, acc_ref = refs
    h = h_ref[...]
    u = _rms_scale(h, g_ref[...]).astype(BF16)
    for c in range(n_chunks):
        a = jnp.dot(u, w1_ref[:, c * fc:(c + 1) * fc], preferred_element_type=F32)
        b = jnp.dot(u, w3_ref[:, c * fc:(c + 1) * fc], preferred_element_type=F32)
        gate = (_silu(a) * b).astype(BF16)
        d = jnp.dot(gate, w2_ref[c * fc:(c + 1) * fc, :], preferred_element_type=F32)
        if c == 0:
            acc_ref[...] = d
        else:
            acc_ref[...] += d
    out = h + FFN_HALF * acc_ref[...]
    o_ref[...] = out
    if with_kv:
        kw = k_ref.shape[1]
        u2 = _rms_scale(out, kvg_ref[...]).astype(BF16)
        kv = jnp.dot(u2, wkv_ref[...], preferred_element_type=F32)
        k_ref[...] = _head_norm_k(kv[:, :kw], kn_ref[...]).astype(k_ref.dtype)
        v_ref[...] = kv[:, kw:].astype(v_ref.dtype)


def _ffn_half_step(h, g, w1, w3, w2, layer, half, kv_params=None):
    n, d = h.shape
    f = w1.shape[-1]
    tm = min(FFN_TOKEN_TILE, n)
    fc = FFN_CHUNK
    assert n % tm == 0 and f % fc == 0
    with_kv = kv_params is not None

    def wsel(*_):
        return (layer, half, 0, 0)

    one = pl.Buffered(1)
    row = pl.BlockSpec((tm, d), lambda i: (i, 0))
    in_specs = [
        row,
        _const_spec((1, d)),
        pl.BlockSpec((None, None, d, f), wsel, pipeline_mode=one),
        pl.BlockSpec((None, None, d, f), wsel, pipeline_mode=one),
        pl.BlockSpec((None, None, f, d), wsel, pipeline_mode=one),
    ]
    args = [h, g.reshape(1, d), w1, w3, w2]
    out_shape = jax.ShapeDtypeStruct((n, d), F32)
    out_specs = row
    if with_kv:
        kv_norm, w_kv16, k_norm = kv_params
        kw = ATT_KV_HEADS * ATT_HEAD_DIM
        assert w_kv16.shape == (d, 2 * kw)
        kn = jnp.tile(k_norm.astype(F32), ATT_KV_HEADS).reshape(1, kw)
        in_specs += [_const_spec((1, d)), _const_spec((d, 2 * kw)), _const_spec((1, kw))]
        args += [kv_norm.reshape(1, d), w_kv16, kn]
        kv_spec = pl.BlockSpec((tm, kw), lambda i: (i, 0))
        out_shape = (out_shape, jax.ShapeDtypeStruct((n, kw), BF16), jax.ShapeDtypeStruct((n, kw), BF16))
        out_specs = (row, kv_spec, kv_spec)
    return pl.pallas_call(
        functools.partial(_ffn_kernel, n_chunks=f // fc, fc=fc, with_kv=with_kv),
        out_shape=out_shape,
        grid=(n // tm,),
        in_specs=in_specs,
        out_specs=out_specs,
        scratch_shapes=[pltpu.VMEM((tm, d), F32)],
        compiler_params=pltpu.CompilerParams(
            dimension_semantics=("arbitrary",), vmem_limit_bytes=V7X_VMEM_LIMIT_BYTES),
        name="ffn_kv_half_step" if with_kv else "ffn_half_step",
    )(*args)


def _pair_cols(v, r0, lane_lo):
    return jnp.where(lane_lo, v[:, r0:r0 + 1], v[:, r0 + 1:r0 + 2])


def _mamba_kernel(h_ref, g_ref, wz_ref, wx_ref, wdt_ref, wdtT_ref, cw_ref, cb_ref,
                  dtb_ref, dtbT_ref, alog_ref, alogT_ref, dskip_ref, gn_ref, wo_ref,
                  o_ref, xpad_ref, xs_ref, state_ref, y_ref, *, n_heads):
    L = h_ref.shape[0]
    ls = SSM_SUBCHUNK
    d_inner = wz_ref.shape[1]
    hp = SSM_HEAD_DIM
    pw = 2 * hp
    n_slab = xpad_ref.shape[0]
    n_xs = d_inner // pw
    heads_per_group = n_heads // SSM_GROUPS
    pairs_per_group = heads_per_group // 2
    pad = V7X_SUBLANES

    @pl.when(pl.program_id(1) == 0)
    def _():
        state_ref[...] = jnp.zeros_like(state_ref)
        xpad_ref[:, 0:pad, :] = jnp.zeros((n_slab, pad, pw), F32)

    h = h_ref[...]
    u = _rms_scale(h, g_ref[...]).astype(BF16)

    z = jnp.dot(u, wz_ref[...], preferred_element_type=F32)
    for j in range(0, n_slab, 2):
        xj = jnp.dot(u, wx_ref[:, j * pw:(j + 2) * pw], preferred_element_type=F32)
        xpad_ref[j, pad:pad + L, :] = xj[:, :pw]
        xpad_ref[j + 1, pad:pad + L, :] = xj[:, pw:]
    dt_raw = jnp.dot(u, wdt_ref[...], preferred_element_type=F32)[:, :n_heads]
    dt_rawT = lax.dot_general(wdtT_ref[...], u, (((1,), (1,)), ((), ())),
                              preferred_element_type=F32)

    b16 = []
    c16 = []
    for j in range(n_slab):
        acc = cb_ref[:, j * pw:(j + 1) * pw]
        for k in range(SSM_CONV):
            off = pad - (SSM_CONV - 1) + k
            acc = acc + cw_ref[k:k + 1, j * pw:(j + 1) * pw] * xpad_ref[j, off:off + L, :]
        xpad_ref[j, 0:pad, :] = xpad_ref[j, L:L + pad, :]
        act = _silu(acc)
        if j < n_xs:
            xs_ref[j] = act
        elif j < n_xs + SSM_GROUPS:
            b16.append(act.astype(BF16))
        else:
            c16.append(act.astype(BF16))

    def softplus(x):
        return jnp.maximum(x, 0.0) + jnp.log1p(jnp.exp(-jnp.abs(x)))

    dt = softplus(dt_raw + dtb_ref[...])
    dtT = softplus(dt_rawT + dtbT_ref[...])
    da = dt * (-jnp.exp(alog_ref[...]))
    daT = dtT * (-jnp.exp(alogT_ref[...]))

    ri = lax.broadcasted_iota(jnp.int32, (ls, ls), 0)
    ci = lax.broadcasted_iota(jnp.int32, (ls, ls), 1)
    causal = ri >= ci
    tri = jnp.where(causal, 1.0, 0.0).astype(F32)
    triT = jnp.where(ci >= ri, 1.0, 0.0).astype(F32)
    lane_lo = lax.broadcasted_iota(jnp.int32, (1, pw), 1) < hp

    for sc in range(L // ls):
        rows = slice(sc * ls, (sc + 1) * ls)
        acum = jnp.dot(tri, da[rows, :], precision=lax.Precision.HIGHEST,
                       preferred_element_type=F32)
        acumT = jnp.dot(daT[:, rows], triT, precision=lax.Precision.HIGHEST,
                        preferred_element_type=F32)
        a_last = acum[ls - 1:ls, :]
        e_acum = jnp.exp(acum)
        e_last = jnp.exp(a_last)
        dt_s = dt[rows, :]
        w_end = jnp.exp(a_last - acum) * dt_s
        acum2 = acum * LOG2E
        acumT2 = acumT * LOG2E
        for g in range(SSM_GROUPS):
            cg = c16[g][rows, :]
            bg = b16[g][rows, :]
            cb = lax.dot_general(cg, bg, (((1,), (1,)), ((), ())),
                                 preferred_element_type=F32)
            st = state_ref[g]
            y_off = jnp.dot(cg, st.astype(BF16), preferred_element_type=F32)
            xw_parts = []
            el_parts = []
            for j in range(pairs_per_group):
                p = g * pairs_per_group + j
                r0 = 2 * p
                xs_pair = xs_ref[p, rows, :]
                xd = xs_pair * _pair_cols(dt_s, r0, lane_lo)
                rhs = jnp.concatenate([jnp.where(lane_lo, xd, 0.0), jnp.where(lane_lo, 0.0, xd)],
                                      axis=0).astype(BF16)
                ms = []
                for r in (r0, r0 + 1):
                    seg = acum2[:, r:r + 1] - acumT2[r:r + 1, :]
                    dec = jnp.exp2(jnp.where(causal, seg, -jnp.inf))
                    ms.append((cb * dec).astype(BF16))
                y_diag = jnp.dot(jnp.concatenate(ms, axis=1), rhs, preferred_element_type=F32)
                y_pair = y_diag + y_off[:, j * pw:(j + 1) * pw] * _pair_cols(e_acum, r0, lane_lo)
                y_ref[rows, p * pw:(p + 1) * pw] = y_pair
                xw_parts.append((xs_pair * _pair_cols(w_end, r0, lane_lo)).astype(BF16))
                el_parts.append(_pair_cols(e_last, r0, lane_lo))
            xw = jnp.concatenate(xw_parts, axis=1)
            el = jnp.concatenate(el_parts, axis=1)
            inc = lax.dot_general(bg, xw, (((0,), (0,)), ((), ())),
                                  preferred_element_type=F32)
            state_ref[g] = st * el + inc

    gw = d_inner // SSM_GROUPS
    slabs_per_group = gw // pw
    parts = []
    for g in range(SSM_GROUPS):
        cols = slice(g * gw, (g + 1) * gw)
        xs_g = jnp.concatenate([xs_ref[g * slabs_per_group + i] for i in range(slabs_per_group)], axis=1)
        yg = (y_ref[:, cols] + dskip_ref[:, cols] * xs_g) * _silu(z[:, cols])
        parts.append(_rms_scale(yg, gn_ref[:, cols]).astype(BF16))
    yn = jnp.concatenate(parts, axis=1)
    o_ref[...] = h + jnp.dot(yn, wo_ref[...], preferred_element_type=F32)


def _mamba_block(h3, norm_g, w_in, conv_w, conv_b, dt_bias, a_log, d_skip, gate_norm, w_out):
    bsz, t, d = h3.shape
    n_heads = dt_bias.shape[0]
    d_inner = n_heads * SSM_HEAD_DIM
    conv_dim = d_inner + 2 * SSM_GROUPS * SSM_STATE
    pw = 2 * SSM_HEAD_DIM
    assert w_in.shape == (d, d_inner + conv_dim + n_heads)
    assert n_heads % (2 * SSM_GROUPS) == 0 and n_heads <= V7X_LANES
    assert pw == V7X_LANES and SSM_STATE == V7X_LANES and (conv_dim // pw) % 2 == 0
    L = SSM_TOKEN_TILE
    assert t % L == 0 and L % SSM_SUBCHUNK == 0
    hpg = n_heads // SSM_GROUPS

    w_in16 = w_in.astype(BF16)
    wz = w_in16[:, :d_inner]
    wx = w_in16[:, d_inner:d_inner + conv_dim]
    wdt = w_in16[:, d_inner + conv_dim:]
    wdt_pad = jnp.pad(wdt, ((0, 0), (0, V7X_LANES - n_heads)))
    wdtT = wdt.T
    dskip_row = jnp.repeat(d_skip.astype(F32), SSM_HEAD_DIM).reshape(1, d_inner)

    arrays = [
        norm_g.reshape(1, d),
        wz, wx, wdt_pad, wdtT,
        conv_w.astype(F32), conv_b.reshape(1, conv_dim).astype(F32),
        dt_bias.reshape(1, n_heads).astype(F32),
        dt_bias.reshape(n_heads, 1).astype(F32),
        a_log.reshape(1, n_heads).astype(F32),
        a_log.reshape(n_heads, 1).astype(F32),
        dskip_row,
        gate_norm.reshape(1, d_inner).astype(F32),
        w_out.astype(BF16),
    ]
    in_specs = [pl.BlockSpec((None, L, d), lambda b, c: (b, c, 0))]
    in_specs += [_const_spec(a.shape) for a in arrays]
    return pl.pallas_call(
        functools.partial(_mamba_kernel, n_heads=n_heads),
        out_shape=jax.ShapeDtypeStruct((bsz, t, d), F32),
        grid=(bsz, t // L),
        in_specs=in_specs,
        out_specs=pl.BlockSpec((None, L, d), lambda b, c: (b, c, 0)),
        scratch_shapes=[
            pltpu.VMEM((conv_dim // pw, L + V7X_SUBLANES, pw), F32),
            pltpu.VMEM((d_inner // pw, L, pw), F32),
            pltpu.VMEM((SSM_GROUPS, SSM_STATE, hpg * SSM_HEAD_DIM), F32),
            pltpu.VMEM((L, d_inner), F32),
        ],
        compiler_params=pltpu.CompilerParams(
            dimension_semantics=("arbitrary", "arbitrary"),
            vmem_limit_bytes=V7X_VMEM_LIMIT_BYTES),
        name="mamba2_block",
    )(h3, *arrays)


def _t5_bucket_table(blk):
    qi = jnp.arange(blk)[:, None] + blk
    kj = jnp.arange(2 * blk)[None, :]
    dist = qi - kj
    n = jnp.maximum(dist, 0)
    max_exact = REL_BUCKETS // 2
    nf = jnp.maximum(n, 1).astype(F32)
    large = max_exact + (jnp.log(nf / max_exact) / math.log(ATT_WINDOW / max_exact)
                         * (REL_BUCKETS - max_exact)).astype(jnp.int32)
    large = jnp.minimum(large, REL_BUCKETS - 1)
    bucket = jnp.where(n < max_exact, n, large)
    in_window = (dist >= 0) & (dist < ATT_WINDOW)
    return jnp.where(in_window, bucket, -1).astype(jnp.int32)


def _attn_kernel(relb_ref, sink_ref, h_ref, g_ref, wq_ref, qn_ref, kp_ref, kc_ref, vp_ref, vc_ref,
                 bucket_ref, wo_ref, o_ref, bias_ref, *, n_heads):
    tq = h_ref.shape[0]
    blk = ATT_WINDOW
    hd = ATT_HEAD_DIM
    pw = 2 * hd
    group = n_heads // ATT_KV_HEADS
    pairs = group // 2
    first = (pl.program_id(0) == 0) & (pl.program_id(1) == 0)

    @pl.when(first)
    def _():
        bias_ref[...] = jnp.full(bias_ref.shape, -jnp.inf, F32)
        bucket = bucket_ref[...]

        def fill(b, carry):
            hit = bucket == b
            for hh in range(n_heads):
                bias_ref[hh] = jnp.where(hit, relb_ref[b, hh] * LOG2E, bias_ref[hh])
            return carry

        lax.fori_loop(0, REL_BUCKETS, fill, 0)

    h = h_ref[...]
    u = _rms_scale(h, g_ref[...]).astype(BF16)
    q = jnp.dot(u, wq_ref[...], preferred_element_type=F32)
    q16 = q.astype(BF16)

    lane_lo = lax.broadcasted_iota(jnp.int32, (1, pw), 1) < hd

    q_scale = []
    for p in range(n_heads // 2):
        qsq = q[:, p * pw:(p + 1) * pw]
        qsq = qsq * qsq
        s_lo = jnp.sum(jnp.where(lane_lo, qsq, 0.0), axis=-1, keepdims=True)
        s_hi = jnp.sum(jnp.where(lane_lo, 0.0, qsq), axis=-1, keepdims=True)
        for ssq in (s_lo, s_hi):
            q_scale.append(lax.rsqrt(ssq * (1.0 / hd) + EPS) * (hd ** -0.5 * LOG2E))

    kall = jnp.concatenate([kp_ref[...], kc_ref[...]], axis=0).astype(F32) * qn_ref[...]
    vall = jnp.concatenate([vp_ref[...], vc_ref[...]], axis=0).astype(F32)
    kswap = pltpu.roll(kall, hd, 1)
    vswap = pltpu.roll(vall, hd, 1)
    k_slots = []
    v_slots = []
    for kvh in range(ATT_KV_HEADS):
        src_k, alt_k = (kall, kswap) if kvh == 0 else (kswap, kall)
        src_v, alt_v = (vall, vswap) if kvh == 0 else (vswap, vall)
        k_slots.append((jnp.where(lane_lo, src_k, 0.0).astype(BF16), jnp.where(lane_lo, 0.0, alt_k).astype(BF16)))
        v_slots.append((jnp.where(lane_lo, src_v, 0.0).astype(BF16), jnp.where(lane_lo, 0.0, alt_v).astype(BF16)))

    kj = lax.broadcasted_iota(jnp.int32, (1, 2 * blk), 1)
    key_ok = kj >= jnp.where(pl.program_id(1) > 0, 0, blk)

    o_rows = []
    for r in range(tq // blk):
        qrows = slice(r * blk, (r + 1) * blk)
        krows = slice(r * blk, r * blk + 2 * blk)
        out_pairs = []
        for kvh in range(ATT_KV_HEADS):
            q_stack = jnp.concatenate(
                [q16[qrows, (kvh * pairs + j) * pw:(kvh * pairs + j + 1) * pw] for j in range(pairs)],
                axis=0)
            s_slots = [lax.dot_general(q_stack, ks[krows, :], (((1,), (1,)), ((), ())),
                                       preferred_element_type=F32) for ks in k_slots[kvh]]
            v_stack = jnp.concatenate([vs[krows, :] for vs in v_slots[kvh]], axis=0)
            for j in range(pairs):
                probs = []
                rinv = []
                for e in range(2):
                    hh = kvh * group + 2 * j + e
                    s = s_slots[e][j * blk:(j + 1) * blk, :] * q_scale[hh][qrows, :] + bias_ref[hh]
                    if r == 0:
                        s = jnp.where(key_ok, s, -jnp.inf)
                    sink = sink_ref[hh] * LOG2E
                    m = jnp.maximum(jnp.max(s, axis=-1, keepdims=True), sink)
                    pexp = jnp.exp2(s - m)
                    denom = jnp.sum(pexp, axis=-1, keepdims=True) + jnp.exp2(sink - m)
                    probs.append(pexp.astype(BF16))
                    rinv.append(1.0 / denom)
                pv = jnp.dot(jnp.concatenate(probs, axis=1), v_stack, preferred_element_type=F32)
                out_pairs.append((pv * jnp.where(lane_lo, rinv[0], rinv[1])).astype(BF16))
        o_rows.append(jnp.concatenate(out_pairs, axis=1))
    o = jnp.concatenate(o_rows, axis=0)
    o_ref[...] = h + jnp.dot(o, wo_ref[...], preferred_element_type=F32)


def _attention_block(h3, k, v, norm_g, w_q, q_norm, sinks, rel_bias, w_o):
    bsz, t, d = h3.shape
    n_heads = sinks.shape[0]
    hd = ATT_HEAD_DIM
    kw = ATT_KV_HEADS * hd
    blk = ATT_WINDOW
    assert w_q.shape == (d, n_heads * hd) and w_o.shape == (n_heads * hd, d)
    assert n_heads % (2 * ATT_KV_HEADS) == 0 and kw == V7X_LANES
    tq = min(ATT_TOKEN_TILE, t)
    assert t % tq == 0 and tq % blk == 0
    ratio = tq // blk
    bucket = _t5_bucket_table(blk)
    qn = jnp.tile(q_norm.astype(F32), ATT_KV_HEADS).reshape(1, kw)

    def prev_blk(b, i, *_):
        return (b, jnp.maximum(i * ratio - 1, 0), 0)

    def cur_blk(b, i, *_):
        return (b, i, 0)

    def const2(b, i, *_):
        return (0, 0)

    one = pl.Buffered(1)
    grid_spec = pltpu.PrefetchScalarGridSpec(
        num_scalar_prefetch=2,
        grid=(bsz, t // tq),
        in_specs=[
            pl.BlockSpec((None, tq, d), cur_blk),
            pl.BlockSpec((1, d), const2, pipeline_mode=one),
            pl.BlockSpec((d, n_heads * hd), const2, pipeline_mode=one),
            pl.BlockSpec((1, kw), const2, pipeline_mode=one),
            pl.BlockSpec((None, blk, kw), prev_blk),
            pl.BlockSpec((None, tq, kw), cur_blk),
            pl.BlockSpec((None, blk, kw), prev_blk),
            pl.BlockSpec((None, tq, kw), cur_blk),
            pl.BlockSpec((blk, 2 * blk), const2, pipeline_mode=one),
            pl.BlockSpec((n_heads * hd, d), const2, pipeline_mode=one),
        ],
        out_specs=pl.BlockSpec((None, tq, d), cur_blk),
        scratch_shapes=[pltpu.VMEM((n_heads, blk, 2 * blk), F32)],
    )
    return pl.pallas_call(
        functools.partial(_attn_kernel, n_heads=n_heads),
        out_shape=jax.ShapeDtypeStruct((bsz, t, d), F32),
        grid_spec=grid_spec,
        compiler_params=pltpu.CompilerParams(
            dimension_semantics=("arbitrary", "arbitrary"),
            vmem_limit_bytes=V7X_VMEM_LIMIT_BYTES),
        name="swa_block",
    )(rel_bias.astype(F32), sinks.astype(F32), h3, norm_g.reshape(1, d), w_q.astype(BF16), qn,
      k, k, v, v, bucket, w_o.astype(BF16))


def kernel(x, ffn_norm, ffn_w1, ffn_w3, ffn_w2, ssm_norm, ssm_w_in, ssm_conv_w, ssm_conv_b, ssm_dt_bias, ssm_a_log, ssm_d, ssm_gate_norm, ssm_w_out, kv_norm, w_kv, k_norm, attn_norm, w_q, q_norm, sinks, w_o, rel_bias):
    bsz, t, d = x.shape
    depth = ffn_norm.shape[0]
    n_a = ssm_norm.shape[0]
    n = bsz * t
    kw = ATT_KV_HEADS * ATT_HEAD_DIM
    w1 = ffn_w1.astype(BF16)
    w3 = ffn_w3.astype(BF16)
    w2 = ffn_w2.astype(BF16)

    h = x
    k_shared = v_shared = None
    for layer in range(depth):
        h = _ffn_half_step(h.reshape(n, d), ffn_norm[layer, 0], w1, w3, w2, layer, 0).reshape(bsz, t, d)
        if layer < n_a:
            i = layer
            h = _mamba_block(h, ssm_norm[i], ssm_w_in[i], ssm_conv_w[i], ssm_conv_b[i], ssm_dt_bias[i],
                             ssm_a_log[i], ssm_d[i], ssm_gate_norm[i], ssm_w_out[i])
        else:
            j = layer - n_a
            h = _attention_block(h, k_shared, v_shared, attn_norm[j], w_q[j], q_norm[j], sinks[j],
                                 rel_bias, w_o[j])
        if layer == n_a - 1:
            h2, k2, v2 = _ffn_half_step(h.reshape(n, d), ffn_norm[layer, 1], w1, w3, w2, layer, 1,
                                        kv_params=(kv_norm, w_kv.astype(BF16), k_norm))
            k_shared = k2.reshape(bsz, t, kw)
            v_shared = v2.reshape(bsz, t, kw)
        else:
            h2 = _ffn_half_step(h.reshape(n, d), ffn_norm[layer, 1], w1, w3, w2, layer, 1)
        h = h2.reshape(bsz, t, d)
    return h
```

```python
import functools
import math

import jax
import jax.numpy as jnp
from jax import lax
from jax.experimental import pallas as pl
from jax.experimental.pallas import tpu as pltpu

F32 = jnp.float32
BF16 = jnp.bfloat16

EPS = 1e-6
FFN_HALF = 0.5
LOG2E = 1.4426950408889634

SSM_HEAD_DIM = 64
SSM_GROUPS = 4
SSM_STATE = 128
SSM_CONV = 4
ATT_HEAD_DIM = 64
ATT_KV_HEADS = 2
ATT_WINDOW = 128
REL_BUCKETS = 32

V7X_LANES = 128
V7X_SUBLANES = 8
V7X_VMEM_LIMIT_BYTES = 56 * 1024 * 1024

FFN_TOKEN_TILE = 512
FFN_CHUNK = 256
SSM_TOKEN_TILE = 256
SSM_SUBCHUNK = 128
ATT_TOKEN_TILE = 256


def _rms_scale(x, g):
    ms = jnp.mean(x * x, axis=-1, keepdims=True)
    return x * lax.rsqrt(ms + EPS) * g


def _sigmoid(x):
    return 0.5 * jnp.tanh(0.5 * x) + 0.5


def _silu(x):
    return x * _sigmoid(x)


def _const_spec(shape):
    zeros = (0,) * len(shape)
    return pl.BlockSpec(shape, lambda *_: zeros, pipeline_mode=pl.Buffered(1))


def _pair_lane_mask():
    return lax.broadcasted_iota(jnp.int32, (1, 2 * ATT_HEAD_DIM), 1) < ATT_HEAD_DIM


def _pair_rms_scale(x, lane_lo, post):
    xsq = x * x
    s_lo = jnp.sum(jnp.where(lane_lo, xsq, 0.0), axis=-1, keepdims=True)
    s_hi = jnp.sum(jnp.where(lane_lo, 0.0, xsq), axis=-1, keepdims=True)
    inv = 1.0 / ATT_HEAD_DIM
    return jnp.where(lane_lo, lax.rsqrt(s_lo * inv + EPS) * post, lax.rsqrt(s_hi * inv + EPS) * post)


def _ffn_kernel(*refs, n_chunks, fc, with_kv):
    if with_kv:
        h_ref, g_ref, w1_ref, w3_ref, w2_ref, kvg_ref, wkv_ref, kn_ref, o_ref, k_ref, v_ref, acc_ref = refs
    else:
        h_ref, g_ref, w1_ref, w3_ref, w2_ref, o_ref, acc_ref = refs
    h = h_ref[...]
    u = _rms_scale(h, g_ref[...]).astype(BF16)
    for c in range(n_chunks):
        a = jnp.dot(u, w1_ref[:, c * fc:(c + 1) * fc], preferred_element_type=F32)
        b = jnp.dot(u, w3_ref[:, c * fc:(c + 1) * fc], preferred_element_type=F32)
        gate = (_silu(a) * b).astype(BF16)
        d = jnp.dot(gate, w2_ref[c * fc:(c + 1) * fc, :], preferred_element_type=F32)
        if c == 0:
            acc_ref[...] = d
        else:
            acc_ref[...] += d
    out = h + FFN_HALF * acc_ref[...]
    o_ref[...] = out
    if with_kv:
        kw = k_ref.shape[1]
        u2 = _rms_scale(out, kvg_ref[...]).astype(BF16)
        kv = jnp.dot(u2, wkv_ref[...], preferred_element_type=F32)
        k = kv[:, :kw]
        k_ref[...] = (k * _pair_rms_scale(k, _pair_lane_mask(), 1.0) * kn_ref[...]).astype(k_ref.dtype)
        v_ref[...] = kv[:, kw:].astype(v_ref.dtype)


def _ffn_half_step(h, g, w1, w3, w2, layer, half, kv_params=None):
    n, d = h.shape
    f = w1.shape[-1]
    tm = min(FFN_TOKEN_TILE, n)
    fc = FFN_CHUNK
    assert n % tm == 0 and f % fc == 0
    with_kv = kv_params is not None

    def wsel(*_):
        return (layer, half, 0, 0)

    one = pl.Buffered(1)
    row = pl.BlockSpec((tm, d), lambda i: (i, 0))
    in_specs = [
        row,
        _const_spec((1, d)),
        pl.BlockSpec((None, None, d, f), wsel, pipeline_mode=one),
        pl.BlockSpec((None, None, d, f), wsel, pipeline_mode=one),
        pl.BlockSpec((None, None, f, d), wsel, pipeline_mode=one),
    ]
    args = [h, g.reshape(1, d), w1, w3, w2]
    out_shape = jax.ShapeDtypeStruct((n, d), F32)
    out_specs = row
    if with_kv:
        kv_norm, w_kv16, k_norm = kv_params
        kw = ATT_KV_HEADS * ATT_HEAD_DIM
        assert w_kv16.shape == (d, 2 * kw) and kw == V7X_LANES
        kn = jnp.tile(k_norm.astype(F32), ATT_KV_HEADS).reshape(1, kw)
        in_specs += [_const_spec((1, d)), _const_spec((d, 2 * kw)), _const_spec((1, kw))]
        args += [kv_norm.reshape(1, d), w_kv16, kn]
        kv_spec = pl.BlockSpec((tm, kw), lambda i: (i, 0))
        out_shape = (out_shape, jax.ShapeDtypeStruct((n, kw), BF16), jax.ShapeDtypeStruct((n, kw), BF16))
        out_specs = (row, kv_spec, kv_spec)
    return pl.pallas_call(
        functools.partial(_ffn_kernel, n_chunks=f // fc, fc=fc, with_kv=with_kv),
        out_shape=out_shape,
        grid=(n // tm,),
        in_specs=in_specs,
        out_specs=out_specs,
        scratch_shapes=[pltpu.VMEM((tm, d), F32)],
        compiler_params=pltpu.CompilerParams(
            dimension_semantics=("arbitrary",), vmem_limit_bytes=V7X_VMEM_LIMIT_BYTES),
        name="ffn_kv_half_step" if with_kv else "ffn_half_step",
    )(*args)


N_HANDOVER = 8


def _pair_cols(v, r0, lane_lo):
    return jnp.where(lane_lo, v[:, r0:r0 + 1], v[:, r0 + 1:r0 + 2])


def _mamba_stage_a(h_ref, g_ref, wz_ref, wx_ref, wdt_ref, wdtT_ref, cw_ref, cb_ref,
                   dtb_ref, dtbT_ref, alog_ref, alogT_ref, xpad_ref, bufs, *, n_heads):
    hs_ref, z_ref, xs_ref, b_ref, c_ref, dt_ref, da_ref, daT_ref = bufs
    L = h_ref.shape[0]
    d_inner = z_ref.shape[1]
    pw = 2 * SSM_HEAD_DIM
    n_slab = xpad_ref.shape[0]
    n_xs = xs_ref.shape[0]
    pad = V7X_SUBLANES
    zc = 4 * pw

    h = h_ref[...]
    hs_ref[...] = h
    u = _rms_scale(h, g_ref[...]).astype(BF16)
    yield

    def softplus(x):
        return jnp.maximum(x, 0.0) + jnp.log1p(jnp.exp(-jnp.abs(x)))

    dt_raw = jnp.dot(u, wdt_ref[...], preferred_element_type=F32)[:, :n_heads]
    dt_rawT = lax.dot_general(wdtT_ref[...], u, (((1,), (1,)), ((), ())),
                              preferred_element_type=F32)
    dt = softplus(dt_raw + dtb_ref[...])
    dtT = softplus(dt_rawT + dtbT_ref[...])
    dt_ref[...] = dt
    da_ref[...] = dt * (-jnp.exp(alog_ref[...]))
    daT_ref[...] = dtT * (-jnp.exp(alogT_ref[...]))
    yield

    for j0 in range(0, n_slab, 2):
        xj = jnp.dot(u, wx_ref[:, j0 * pw:(j0 + 2) * pw], preferred_element_type=F32)
        xpad_ref[j0, pad:pad + L, :] = xj[:, :pw]
        xpad_ref[j0 + 1, pad:pad + L, :] = xj[:, pw:]
        for j in (j0, j0 + 1):
            acc = cb_ref[:, j * pw:(j + 1) * pw]
            for k in range(SSM_CONV):
                off = pad - (SSM_CONV - 1) + k
                acc = acc + cw_ref[k:k + 1, j * pw:(j + 1) * pw] * xpad_ref[j, off:off + L, :]
            xpad_ref[j, 0:pad, :] = xpad_ref[j, L:L + pad, :]
            act = _silu(acc)
            if j < n_xs:
                xs_ref[j] = act
            elif j < n_xs + SSM_GROUPS:
                b_ref[j - n_xs] = act.astype(BF16)
            else:
                c_ref[j - n_xs - SSM_GROUPS] = act.astype(BF16)
        yield

    for c0 in range(0, d_inner, zc):
        z_ref[:, c0:c0 + zc] = jnp.dot(u, wz_ref[:, c0:c0 + zc], preferred_element_type=F32)
        yield


def _mamba_stage_b(bufs, dskip_ref, gn_ref, wo_ref, o_ref, state_ref, y_ref, *, n_heads):
    hs_ref, z_ref, xs_ref, b_ref, c_ref, dt_ref, da_ref, daT_ref = bufs
    L = hs_ref.shape[0]
    ls = SSM_SUBCHUNK
    d_inner = z_ref.shape[1]
    hp = SSM_HEAD_DIM
    pw = 2 * hp
    pairs_per_group = n_heads // SSM_GROUPS // 2

    ri = lax.broadcasted_iota(jnp.int32, (ls, ls), 0)
    ci = lax.broadcasted_iota(jnp.int32, (ls, ls), 1)
    causal = ri >= ci
    tri = jnp.where(causal, 1.0, 0.0).astype(F32)
    triT = jnp.where(ci >= ri, 1.0, 0.0).astype(F32)
    lane_lo = lax.broadcasted_iota(jnp.int32, (1, pw), 1) < hp

    for sc in range(L // ls):
        rows = slice(sc * ls, (sc + 1) * ls)
        acum = jnp.dot(tri, da_ref[rows, :], precision=lax.Precision.HIGHEST,
                       preferred_element_type=F32)
        acumT = jnp.dot(daT_ref[:, rows], triT, precision=lax.Precision.HIGHEST,
                        preferred_element_type=F32)
        a_last = acum[ls - 1:ls, :]
        e_acum = jnp.exp(acum)
        e_last = jnp.exp(a_last)
        dt_s = dt_ref[rows, :]
        w_end = jnp.exp(a_last - acum) * dt_s
        acum2 = acum * LOG2E
        acumT2 = acumT * LOG2E
        yield
        for g in range(SSM_GROUPS):
            cg = c_ref[g, rows, :]
            bg = b_ref[g, rows, :]
            cb = lax.dot_general(cg, bg, (((1,), (1,)), ((), ())),
                                 preferred_element_type=F32)
            st = state_ref[g]
            y_off = jnp.dot(cg, st.astype(BF16), preferred_element_type=F32)
            xw_parts = []
            el_parts = []
            for j in range(pairs_per_group):
                p = g * pairs_per_group + j
                r0 = 2 * p
                xs_pair = xs_ref[p, rows, :]
                xd = xs_pair * _pair_cols(dt_s, r0, lane_lo)
                rhs = jnp.concatenate([jnp.where(lane_lo, xd, 0.0), jnp.where(lane_lo, 0.0, xd)],
                                      axis=0).astype(BF16)
                ms = []
                for r in (r0, r0 + 1):
                    seg = acum2[:, r:r + 1] - acumT2[r:r + 1, :]
                    dec = jnp.exp2(jnp.where(causal, seg, -jnp.inf))
                    ms.append((cb * dec).astype(BF16))
                y_diag = jnp.dot(jnp.concatenate(ms, axis=1), rhs, preferred_element_type=F32)
                y_pair = y_diag + y_off[:, j * pw:(j + 1) * pw] * _pair_cols(e_acum, r0, lane_lo)
                y_ref[rows, p * pw:(p + 1) * pw] = y_pair
                xw_parts.append((xs_pair * _pair_cols(w_end, r0, lane_lo)).astype(BF16))
                el_parts.append(_pair_cols(e_last, r0, lane_lo))
                if j % 2 == 1 and j + 1 < pairs_per_group:
                    yield
            xw = jnp.concatenate(xw_parts, axis=1)
            el = jnp.concatenate(el_parts, axis=1)
            inc = lax.dot_general(bg, xw, (((0,), (0,)), ((), ())),
                                  preferred_element_type=F32)
            state_ref[g] = st * el + inc
            yield

    gw = d_inner // SSM_GROUPS
    slabs_per_group = gw // pw
    for g in range(SSM_GROUPS):
        cols = slice(g * gw, (g + 1) * gw)
        xs_g = jnp.concatenate([xs_ref[g * slabs_per_group + i] for i in range(slabs_per_group)], axis=1)
        yg = (y_ref[:, cols] + dskip_ref[:, cols] * xs_g) * _silu(z_ref[:, cols])
        yn = _rms_scale(yg, gn_ref[:, cols]).astype(BF16)
        part = jnp.dot(yn, wo_ref[cols, :], preferred_element_type=F32)
        if g == 0:
            o_ref[...] = hs_ref[...] + part
        else:
            o_ref[...] += part
        yield


def _interleave(*gens):
    live = list(gens)
    while live:
        for gen in list(live):
            try:
                next(gen)
            except StopIteration:
                live.remove(gen)


def _mamba_kernel(h_ref, g_ref, wz_ref, wx_ref, wdt_ref, wdtT_ref, cw_ref, cb_ref,
                  dtb_ref, dtbT_ref, alog_ref, alogT_ref, dskip_ref, gn_ref, wo_ref,
                  o_ref, xpad_ref, state_ref, y_ref, *handover, n_heads, tiles_per_seq):
    slots = (handover[:N_HANDOVER], handover[N_HANDOVER:])
    s = pl.program_id(0)
    pad = V7X_SUBLANES

    @pl.when(s == 0)
    def _():
        for ref in slots[1]:
            ref[...] = jnp.zeros_like(ref)

    @pl.when(lax.rem(s, tiles_per_seq) == 0)
    def _():
        xpad_ref[:, 0:pad, :] = jnp.zeros((xpad_ref.shape[0], pad, xpad_ref.shape[2]), F32)

    @pl.when((s == 0) | (lax.rem(s - 1, tiles_per_seq) == 0))
    def _():
        state_ref[...] = jnp.zeros_like(state_ref)

    def step(write_slot):
        _interleave(
            _mamba_stage_a(h_ref, g_ref, wz_ref, wx_ref, wdt_ref, wdtT_ref, cw_ref, cb_ref,
                           dtb_ref, dtbT_ref, alog_ref, alogT_ref, xpad_ref, slots[write_slot],
                           n_heads=n_heads),
            _mamba_stage_b(slots[1 - write_slot], dskip_ref, gn_ref, wo_ref, o_ref, state_ref, y_ref,
                           n_heads=n_heads))

    parity = lax.rem(s, 2)

    @pl.when(parity == 0)
    def _():
        step(0)

    @pl.when(parity == 1)
    def _():
        step(1)


def _mamba_block(h3, norm_g, w_in, conv_w, conv_b, dt_bias, a_log, d_skip, gate_norm, w_out):
    bsz, t, d = h3.shape
    n_heads = dt_bias.shape[0]
    d_inner = n_heads * SSM_HEAD_DIM
    conv_dim = d_inner + 2 * SSM_GROUPS * SSM_STATE
    pw = 2 * SSM_HEAD_DIM
    assert w_in.shape == (d, d_inner + conv_dim + n_heads)
    assert n_heads % (2 * SSM_GROUPS) == 0 and n_heads <= V7X_LANES
    assert pw == V7X_LANES and SSM_STATE == V7X_LANES and (conv_dim // pw) % 2 == 0
    L = SSM_TOKEN_TILE
    assert t % L == 0 and L % SSM_SUBCHUNK == 0
    hpg = n_heads // SSM_GROUPS
    tiles_per_seq = t // L
    n_tiles = bsz * tiles_per_seq

    w_in16 = w_in.astype(BF16)
    wz = w_in16[:, :d_inner]
    wx = w_in16[:, d_inner:d_inner + conv_dim]
    wdt = w_in16[:, d_inner + conv_dim:]
    wdt_pad = jnp.pad(wdt, ((0, 0), (0, V7X_LANES - n_heads)))
    wdtT = wdt.T
    dskip_row = jnp.repeat(d_skip.astype(F32), SSM_HEAD_DIM).reshape(1, d_inner)

    arrays = [
        norm_g.reshape(1, d),
        wz, wx, wdt_pad, wdtT,
        conv_w.astype(F32), conv_b.reshape(1, conv_dim).astype(F32),
        dt_bias.reshape(1, n_heads).astype(F32),
        dt_bias.reshape(n_heads, 1).astype(F32),
        a_log.reshape(1, n_heads).astype(F32),
        a_log.reshape(n_heads, 1).astype(F32),
        dskip_row,
        gate_norm.reshape(1, d_inner).astype(F32),
        w_out.astype(BF16),
    ]
    handover = [
        pltpu.VMEM((L, d), F32),
        pltpu.VMEM((L, d_inner), F32),
        pltpu.VMEM((d_inner // pw, L, pw), F32),
        pltpu.VMEM((SSM_GROUPS, L, SSM_STATE), BF16),
        pltpu.VMEM((SSM_GROUPS, L, SSM_STATE), BF16),
        pltpu.VMEM((L, n_heads), F32),
        pltpu.VMEM((L, n_heads), F32),
        pltpu.VMEM((n_heads, L), F32),
    ]
    assert len(handover) == N_HANDOVER
    in_specs = [pl.BlockSpec((L, d), lambda s: (jnp.minimum(s, n_tiles - 1), 0))]
    in_specs += [_const_spec(a.shape) for a in arrays]
    out = pl.pallas_call(
        functools.partial(_mamba_kernel, n_heads=n_heads, tiles_per_seq=tiles_per_seq),
        out_shape=jax.ShapeDtypeStruct((bsz * t, d), F32),
        grid=(n_tiles + 1,),
        in_specs=in_specs,
        out_specs=pl.BlockSpec((L, d), lambda s: (jnp.maximum(s - 1, 0), 0)),
        scratch_shapes=[
            pltpu.VMEM((conv_dim // pw, L + V7X_SUBLANES, pw), F32),
            pltpu.VMEM((SSM_GROUPS, SSM_STATE, hpg * SSM_HEAD_DIM), F32),
            pltpu.VMEM((L, d_inner), F32),
        ] + handover + handover,
        compiler_params=pltpu.CompilerParams(
            dimension_semantics=("arbitrary",),
            vmem_limit_bytes=V7X_VMEM_LIMIT_BYTES),
        name="mamba2_block",
    )(h3.reshape(bsz * t, d), *arrays)
    return out.reshape(bsz, t, d)


def _t5_bucket_table(blk):
    qi = jnp.arange(blk)[:, None] + blk
    kj = jnp.arange(2 * blk)[None, :]
    dist = qi - kj
    n = jnp.maximum(dist, 0)
    max_exact = REL_BUCKETS // 2
    nf = jnp.maximum(n, 1).astype(F32)
    large = max_exact + (jnp.log(nf / max_exact) / math.log(ATT_WINDOW / max_exact)
                         * (REL_BUCKETS - max_exact)).astype(jnp.int32)
    large = jnp.minimum(large, REL_BUCKETS - 1)
    bucket = jnp.where(n < max_exact, n, large)
    in_window = (dist >= 0) & (dist < ATT_WINDOW)
    return jnp.where(in_window, bucket, -1).astype(jnp.int32)


def _attn_kernel(relb_ref, sink_ref, h_ref, g_ref, wq_ref, qn_ref, kp_ref, kc_ref, vp_ref, vc_ref,
                 bucket_ref, wo_ref, o_ref, bias_ref, *, n_heads):
    tq = h_ref.shape[0]
    blk = ATT_WINDOW
    hd = ATT_HEAD_DIM
    pw = 2 * hd
    group = n_heads // ATT_KV_HEADS
    pairs = group // 2
    first = (pl.program_id(0) == 0) & (pl.program_id(1) == 0)

    @pl.when(first)
    def _():
        bias_ref[...] = jnp.full(bias_ref.shape, -jnp.inf, F32)
        bucket = bucket_ref[...]

        def fill(b, carry):
            hit = bucket == b
            for hh in range(n_heads):
                bias_ref[hh] = jnp.where(hit, relb_ref[b, hh] * LOG2E, bias_ref[hh])
            return carry

        lax.fori_loop(0, REL_BUCKETS, fill, 0)

    h = h_ref[...]
    u = _rms_scale(h, g_ref[...]).astype(BF16)
    q = jnp.dot(u, wq_ref[...], preferred_element_type=F32)
    lane_lo = _pair_lane_mask()

    q16 = jnp.concatenate(
        [(q[:, p * pw:(p + 1) * pw]
          * _pair_rms_scale(q[:, p * pw:(p + 1) * pw], lane_lo, hd ** -0.5 * LOG2E)).astype(BF16)
         for p in range(n_heads // 2)], axis=1)

    kall = jnp.concatenate([kp_ref[...], kc_ref[...]], axis=0).astype(F32) * qn_ref[...]
    vall = jnp.concatenate([vp_ref[...], vc_ref[...]], axis=0).astype(F32)
    kswap = pltpu.roll(kall, hd, 1)
    vswap = pltpu.roll(vall, hd, 1)
    k_slots = []
    v_slots = []
    for kvh in range(ATT_KV_HEADS):
        src_k, alt_k = (kall, kswap) if kvh == 0 else (kswap, kall)
        src_v, alt_v = (vall, vswap) if kvh == 0 else (vswap, vall)
        k_slots.append((jnp.where(lane_lo, src_k, 0.0).astype(BF16), jnp.where(lane_lo, 0.0, alt_k).astype(BF16)))
        v_slots.append((jnp.where(lane_lo, src_v, 0.0).astype(BF16), jnp.where(lane_lo, 0.0, alt_v).astype(BF16)))

    kj = lax.broadcasted_iota(jnp.int32, (1, 2 * blk), 1)
    key_ok = kj >= jnp.where(pl.program_id(1) > 0, 0, blk)

    o_rows = []
    for r in range(tq // blk):
        qrows = slice(r * blk, (r + 1) * blk)
        krows = slice(r * blk, r * blk + 2 * blk)
        out_pairs = []
        for kvh in range(ATT_KV_HEADS):
            q_stack = jnp.concatenate(
                [q16[qrows, (kvh * pairs + j) * pw:(kvh * pairs + j + 1) * pw] for j in range(pairs)],
                axis=0)
            s_slots = [lax.dot_general(q_stack, ks[krows, :], (((1,), (1,)), ((), ())),
                                       preferred_element_type=F32) for ks in k_slots[kvh]]
            v_stack = jnp.concatenate([vs[krows, :] for vs in v_slots[kvh]], axis=0)
            for j in range(pairs):
                probs = []
                rinv = []
                for e in range(2):
                    hh = kvh * group + 2 * j + e
                    s = s_slots[e][j * blk:(j + 1) * blk, :] + bias_ref[hh]
                    if r == 0:
                        s = jnp.where(key_ok, s, -jnp.inf)
                    sink = sink_ref[hh] * LOG2E
                    m = jnp.maximum(jnp.max(s, axis=-1, keepdims=True), sink)
                    pexp = jnp.exp2(s - m)
                    denom = jnp.sum(pexp, axis=-1, keepdims=True) + jnp.exp2(sink - m)
                    probs.append(pexp.astype(BF16))
                    rinv.append(1.0 / denom)
                pv = jnp.dot(jnp.concatenate(probs, axis=1), v_stack, preferred_element_type=F32)
                out_pairs.append((pv * jnp.where(lane_lo, rinv[0], rinv[1])).astype(BF16))
        o_rows.append(jnp.concatenate(out_pairs, axis=1))
    o = jnp.concatenate(o_rows, axis=0)
    o_ref[...] = h + jnp.dot(o, wo_ref[...], preferred_element_type=F32)


def _attention_block(h3, k, v, norm_g, w_q, q_norm, sinks, rel_bias, w_o):
    bsz, t, d = h3.shape
    n_heads = sinks.shape[0]
    hd = ATT_HEAD_DIM
    kw = ATT_KV_HEADS * hd
    blk = ATT_WINDOW
    assert w_q.shape == (d, n_heads * hd) and w_o.shape == (n_heads * hd, d)
    assert n_heads % (2 * ATT_KV_HEADS) == 0 and kw == V7X_LANES
    tq = min(ATT_TOKEN_TILE, t)
    assert t % tq == 0 and tq % blk == 0
    ratio = tq // blk
    bucket = _t5_bucket_table(blk)
    qn = jnp.tile(q_norm.astype(F32), ATT_KV_HEADS).reshape(1, kw)
    k3 = k.reshape(bsz, t, kw)
    v3 = v.reshape(bsz, t, kw)

    def prev_blk(b, i, *_):
        return (b, jnp.maximum(i * ratio - 1, 0), 0)

    def cur_blk(b, i, *_):
        return (b, i, 0)

    def const2(b, i, *_):
        return (0, 0)

    one = pl.Buffered(1)
    grid_spec = pltpu.PrefetchScalarGridSpec(
        num_scalar_prefetch=2,
        grid=(bsz, t // tq),
        in_specs=[
            pl.BlockSpec((None, tq, d), cur_blk),
            pl.BlockSpec((1, d), const2, pipeline_mode=one),
            pl.BlockSpec((d, n_heads * hd), const2, pipeline_mode=one),
            pl.BlockSpec((1, kw), const2, pipeline_mode=one),
            pl.BlockSpec((None, blk, kw), prev_blk),
            pl.BlockSpec((None, tq, kw), cur_blk),
            pl.BlockSpec((None, blk, kw), prev_blk),
            pl.BlockSpec((None, tq, kw), cur_blk),
            pl.BlockSpec((blk, 2 * blk), const2, pipeline_mode=one),
            pl.BlockSpec((n_heads * hd, d), const2, pipeline_mode=one),
        ],
        out_specs=pl.BlockSpec((None, tq, d), cur_blk),
        scratch_shapes=[pltpu.VMEM((n_heads, blk, 2 * blk), F32)],
    )
    return pl.pallas_call(
        functools.partial(_attn_kernel, n_heads=n_heads),
        out_shape=jax.ShapeDtypeStruct((bsz, t, d), F32),
        grid_spec=grid_spec,
        compiler_params=pltpu.CompilerParams(
            dimension_semantics=("arbitrary", "arbitrary"),
            vmem_limit_bytes=V7X_VMEM_LIMIT_BYTES),
        name="swa_block",
    )(rel_bias.astype(F32), sinks.astype(F32), h3, norm_g.reshape(1, d), w_q.astype(BF16), qn,
      k3, k3, v3, v3, bucket, w_o.astype(BF16))


def kernel(x, ffn_norm, ffn_w1, ffn_w3, ffn_w2, ssm_norm, ssm_w_in, ssm_conv_w, ssm_conv_b, ssm_dt_bias, ssm_a_log, ssm_d, ssm_gate_norm, ssm_w_out, kv_norm, w_kv, k_norm, attn_norm, w_q, q_norm, sinks, w_o, rel_bias):
    bsz, t, d = x.shape
    depth = ffn_norm.shape[0]
    n_a = ssm_norm.shape[0]
    n = bsz * t
    w1 = ffn_w1.astype(BF16)
    w3 = ffn_w3.astype(BF16)
    w2 = ffn_w2.astype(BF16)

    h = x
    k_shared = v_shared = None
    for layer in range(depth):
        h = _ffn_half_step(h.reshape(n, d), ffn_norm[layer, 0], w1, w3, w2, layer, 0).reshape(bsz, t, d)
        if layer < n_a:
            i = layer
            h = _mamba_block(h, ssm_norm[i], ssm_w_in[i], ssm_conv_w[i], ssm_conv_b[i], ssm_dt_bias[i],
                             ssm_a_log[i], ssm_d[i], ssm_gate_norm[i], ssm_w_out[i])
        else:
            j = layer - n_a
            h = _attention_block(h, k_shared, v_shared, attn_norm[j], w_q[j], q_norm[j], sinks[j],
                                 rel_bias, w_o[j])
        if layer == n_a - 1:
            h2, k2, v2 = _ffn_half_step(h.reshape(n, d), ffn_norm[layer, 1], w1, w3, w2, layer, 1,
                                        kv_params=(kv_norm, w_kv.astype(BF16), k_norm))
            k_shared, v_shared = k2, v2
        else:
            h2 = _ffn_half_step(h.reshape(n, d), ffn_norm[layer, 1], w1, w3, w2, layer, 1)
        h = h2.reshape(bsz, t, d)
    return h
```

```python
import functools
import math

import jax
import jax.numpy as jnp
from jax import lax
from jax.experimental import pallas as pl
from jax.experimental.pallas import tpu as pltpu

F32 = jnp.float32
BF16 = jnp.bfloat16

EPS = 1e-6
FFN_HALF = 0.5
LOG2E = 1.4426950408889634

SSM_HEAD_DIM = 64
SSM_GROUPS = 4
SSM_STATE = 128
SSM_CONV = 4
ATT_HEAD_DIM = 64
ATT_KV_HEADS = 2
ATT_WINDOW = 128
REL_BUCKETS = 32

V7X_LANES = 128
V7X_SUBLANES = 8
V7X_VMEM_LIMIT_BYTES = 56 * 1024 * 1024

FFN_TOKEN_TILE = 1024
FFN_CHUNK = 256
SSM_TOKEN_TILE = 256
SSM_SUBCHUNK = 128
ATT_TOKEN_TILE = 512


def _rms_scale(x, g):
    ms = jnp.mean(x * x, axis=-1, keepdims=True)
    return x * lax.rsqrt(ms + EPS) * g


def _sigmoid(x):
    return 0.5 * jnp.tanh(0.5 * x) + 0.5


def _silu(x):
    return x * _sigmoid(x)


def _const_spec(shape):
    zeros = (0,) * len(shape)
    return pl.BlockSpec(shape, lambda *_: zeros, pipeline_mode=pl.Buffered(1))


def _pair_lane_mask():
    return lax.broadcasted_iota(jnp.int32, (1, 2 * ATT_HEAD_DIM), 1) < ATT_HEAD_DIM


def _pair_rms_scale(x, lane_lo, post):
    xsq = x * x
    s_lo = jnp.sum(jnp.where(lane_lo, xsq, 0.0), axis=-1, keepdims=True)
    s_hi = jnp.sum(jnp.where(lane_lo, 0.0, xsq), axis=-1, keepdims=True)
    inv = 1.0 / ATT_HEAD_DIM
    return jnp.where(lane_lo, lax.rsqrt(s_lo * inv + EPS) * post, lax.rsqrt(s_hi * inv + EPS) * post)


def _ffn_kernel(*refs, n_chunks, fc, with_kv):
    if with_kv:
        h_ref, g_ref, w1_ref, w3_ref, w2_ref, kvg_ref, wkv_ref, kn_ref, o_ref, k_ref, v_ref = refs
    else:
        h_ref, g_ref, w1_ref, w3_ref, w2_ref, o_ref = refs
    h = h_ref[...]
    u = _rms_scale(h, g_ref[...]).astype(BF16)
    for c in range(n_chunks):
        cols = slice(c * fc, (c + 1) * fc)
        a = jnp.dot(u, w1_ref[:, cols].astype(BF16), preferred_element_type=F32)
        b = jnp.dot(u, w3_ref[:, cols].astype(BF16), preferred_element_type=F32)
        gate = (_silu(a) * b).astype(BF16)
        d = jnp.dot(gate, w2_ref[cols, :].astype(BF16), preferred_element_type=F32)
        if c == 0:
            o_ref[...] = d
        else:
            o_ref[...] += d
    out = h + FFN_HALF * o_ref[...]
    o_ref[...] = out
    if with_kv:
        kw = k_ref.shape[1]
        u2 = _rms_scale(out, kvg_ref[...]).astype(BF16)
        kv = jnp.dot(u2, wkv_ref[...], preferred_element_type=F32)
        k = kv[:, :kw]
        k_ref[...] = (k * _pair_rms_scale(k, _pair_lane_mask(), 1.0) * kn_ref[...]).astype(k_ref.dtype)
        v_ref[...] = kv[:, kw:].astype(v_ref.dtype)


def _ffn_half_step(h, g, w1, w3, w2, layer, half, kv_params=None):
    n, d = h.shape
    f = w1.shape[-1]
    tm = min(FFN_TOKEN_TILE, n)
    fc = FFN_CHUNK
    assert n % tm == 0 and f % fc == 0
    with_kv = kv_params is not None

    def wsel(*_):
        return (layer, half, 0, 0)

    one = pl.Buffered(1)
    row = pl.BlockSpec((tm, d), lambda i: (i, 0))
    in_specs = [
        row,
        _const_spec((1, d)),
        pl.BlockSpec((None, None, d, f), wsel, pipeline_mode=one),
        pl.BlockSpec((None, None, d, f), wsel, pipeline_mode=one),
        pl.BlockSpec((None, None, f, d), wsel, pipeline_mode=one),
    ]
    args = [h, g.reshape(1, d), w1, w3, w2]
    out_shape = jax.ShapeDtypeStruct((n, d), F32)
    out_specs = row
    if with_kv:
        kv_norm, w_kv16, k_norm = kv_params
        kw = ATT_KV_HEADS * ATT_HEAD_DIM
        assert w_kv16.shape == (d, 2 * kw) and kw == V7X_LANES
        kn = jnp.tile(k_norm.astype(F32), ATT_KV_HEADS).reshape(1, kw)
        in_specs += [_const_spec((1, d)), _const_spec((d, 2 * kw)), _const_spec((1, kw))]
        args += [kv_norm.reshape(1, d), w_kv16, kn]
        kv_spec = pl.BlockSpec((tm, kw), lambda i: (i, 0))
        out_shape = (out_shape, jax.ShapeDtypeStruct((n, kw), BF16), jax.ShapeDtypeStruct((n, kw), BF16))
        out_specs = (row, kv_spec, kv_spec)
    return pl.pallas_call(
        functools.partial(_ffn_kernel, n_chunks=f // fc, fc=fc, with_kv=with_kv),
        out_shape=out_shape,
        grid=(n // tm,),
        in_specs=in_specs,
        out_specs=out_specs,
        compiler_params=pltpu.CompilerParams(
            dimension_semantics=("arbitrary",), vmem_limit_bytes=V7X_VMEM_LIMIT_BYTES),
        name="ffn_kv_half_step" if with_kv else "ffn_half_step",
    )(*args)


N_HANDOVER = 8


def _pair_cols(v, r0, lane_lo):
    return jnp.where(lane_lo, v[:, r0:r0 + 1], v[:, r0 + 1:r0 + 2])


def _mamba_stage_a(h_ref, g_ref, wz_ref, wx_ref, wdt_ref, wdtT_ref, cw_ref, cb_ref,
                   dtb_ref, dtbT_ref, alog_ref, alogT_ref, xpad_ref, bufs, *, n_heads):
    hs_ref, z_ref, xs_ref, b_ref, c_ref, dt_ref, da_ref, daT_ref = bufs
    L = h_ref.shape[0]
    d_inner = z_ref.shape[1]
    pw = 2 * SSM_HEAD_DIM
    n_slab = xpad_ref.shape[0]
    n_xs = xs_ref.shape[0]
    pad = V7X_SUBLANES
    zc = 4 * pw

    h = h_ref[...]
    hs_ref[...] = h
    u = _rms_scale(h, g_ref[...]).astype(BF16)
    yield

    def softplus(x):
        return jnp.maximum(x, 0.0) + jnp.log1p(jnp.exp(-jnp.abs(x)))

    dt_raw = jnp.dot(u, wdt_ref[...], preferred_element_type=F32)[:, :n_heads]
    dt_rawT = lax.dot_general(wdtT_ref[...], u, (((1,), (1,)), ((), ())),
                              preferred_element_type=F32)
    dt = softplus(dt_raw + dtb_ref[...])
    dtT = softplus(dt_rawT + dtbT_ref[...])
    dt_ref[...] = dt
    da_ref[...] = dt * (-jnp.exp(alog_ref[...]))
    daT_ref[...] = dtT * (-jnp.exp(alogT_ref[...]))
    yield

    for j0 in range(0, n_slab, 2):
        xj = jnp.dot(u, wx_ref[:, j0 * pw:(j0 + 2) * pw], preferred_element_type=F32)
        xpad_ref[j0, pad:pad + L, :] = xj[:, :pw]
        xpad_ref[j0 + 1, pad:pad + L, :] = xj[:, pw:]
        for j in (j0, j0 + 1):
            acc = cb_ref[:, j * pw:(j + 1) * pw]
            for k in range(SSM_CONV):
                off = pad - (SSM_CONV - 1) + k
                acc = acc + cw_ref[k:k + 1, j * pw:(j + 1) * pw] * xpad_ref[j, off:off + L, :]
            xpad_ref[j, 0:pad, :] = xpad_ref[j, L:L + pad, :]
            act = _silu(acc)
            if j < n_xs:
                xs_ref[j] = act
            elif j < n_xs + SSM_GROUPS:
                b_ref[j - n_xs] = act.astype(BF16)
            else:
                c_ref[j - n_xs - SSM_GROUPS] = act.astype(BF16)
        yield

    for c0 in range(0, d_inner, zc):
        z_ref[:, c0:c0 + zc] = jnp.dot(u, wz_ref[:, c0:c0 + zc], preferred_element_type=F32)
        yield


def _mamba_stage_b(bufs, dskip_ref, gn_ref, wo_ref, o_ref, state_ref, y_ref, *, n_heads):
    hs_ref, z_ref, xs_ref, b_ref, c_ref, dt_ref, da_ref, daT_ref = bufs
    L = hs_ref.shape[0]
    ls = SSM_SUBCHUNK
    d_inner = z_ref.shape[1]
    hp = SSM_HEAD_DIM
    pw = 2 * hp
    pairs_per_group = n_heads // SSM_GROUPS // 2

    ri = lax.broadcasted_iota(jnp.int32, (ls, ls), 0)
    ci = lax.broadcasted_iota(jnp.int32, (ls, ls), 1)
    causal = ri >= ci
    tri = jnp.where(causal, 1.0, 0.0).astype(F32)
    triT = jnp.where(ci >= ri, 1.0, 0.0).astype(F32)
    lane_lo = lax.broadcasted_iota(jnp.int32, (1, pw), 1) < hp

    for sc in range(L // ls):
        rows = slice(sc * ls, (sc + 1) * ls)
        acum = jnp.dot(tri, da_ref[rows, :], precision=lax.Precision.HIGHEST,
                       preferred_element_type=F32)
        acumT = jnp.dot(daT_ref[:, rows], triT, precision=lax.Precision.HIGHEST,
                        preferred_element_type=F32)
        a_last = acum[ls - 1:ls, :]
        e_acum = jnp.exp(acum)
        e_last = jnp.exp(a_last)
        dt_s = dt_ref[rows, :]
        w_end = jnp.exp(a_last - acum) * dt_s
        acum2 = acum * LOG2E
        acumT2 = acumT * LOG2E
        yield
        for g in range(SSM_GROUPS):
            cg = c_ref[g, rows, :]
            bg = b_ref[g, rows, :]
            cb = lax.dot_general(cg, bg, (((1,), (1,)), ((), ())),
                                 preferred_element_type=F32)
            st = state_ref[g]
            y_off = jnp.dot(cg, st.astype(BF16), preferred_element_type=F32)
            xw_parts = []
            el_parts = []
            for j in range(pairs_per_group):
                p = g * pairs_per_group + j
                r0 = 2 * p
                xs_pair = xs_ref[p, rows, :]
                xd = xs_pair * _pair_cols(dt_s, r0, lane_lo)
                rhs = jnp.concatenate([jnp.where(lane_lo, xd, 0.0), jnp.where(lane_lo, 0.0, xd)],
                                      axis=0).astype(BF16)
                ms = []
                for r in (r0, r0 + 1):
                    seg = acum2[:, r:r + 1] - acumT2[r:r + 1, :]
                    dec = jnp.exp2(jnp.where(causal, seg, -jnp.inf))
                    ms.append((cb * dec).astype(BF16))
                y_diag = jnp.dot(jnp.concatenate(ms, axis=1), rhs, preferred_element_type=F32)
                y_pair = y_diag + y_off[:, j * pw:(j + 1) * pw] * _pair_cols(e_acum, r0, lane_lo)
                y_ref[rows, p * pw:(p + 1) * pw] = y_pair
                xw_parts.append((xs_pair * _pair_cols(w_end, r0, lane_lo)).astype(BF16))
                el_parts.append(_pair_cols(e_last, r0, lane_lo))
                if j % 2 == 1 and j + 1 < pairs_per_group:
                    yield
            xw = jnp.concatenate(xw_parts, axis=1)
            el = jnp.concatenate(el_parts, axis=1)
            inc = lax.dot_general(bg, xw, (((0,), (0,)), ((), ())),
                                  preferred_element_type=F32)
            state_ref[g] = st * el + inc
            yield

    gw = d_inner // SSM_GROUPS
    slabs_per_group = gw // pw
    for g in range(SSM_GROUPS):
        cols = slice(g * gw, (g + 1) * gw)
        xs_g = jnp.concatenate([xs_ref[g * slabs_per_group + i] for i in range(slabs_per_group)], axis=1)
        yg = (y_ref[:, cols] + dskip_ref[:, cols] * xs_g) * _silu(z_ref[:, cols])
        yn = _rms_scale(yg, gn_ref[:, cols]).astype(BF16)
        part = jnp.dot(yn, wo_ref[cols, :], preferred_element_type=F32)
        if g == 0:
            o_ref[...] = hs_ref[...] + part
        else:
            o_ref[...] += part
        yield


def _interleave(*gens):
    live = list(gens)
    while live:
        for gen in list(live):
            try:
                next(gen)
            except StopIteration:
                live.remove(gen)


def _mamba_kernel(h_ref, g_ref, wz_ref, wx_ref, wdt_ref, wdtT_ref, cw_ref, cb_ref,
                  dtb_ref, dtbT_ref, alog_ref, alogT_ref, dskip_ref, gn_ref, wo_ref,
                  o_ref, xpad_ref, state_ref, y_ref, *handover, n_heads, tiles_per_seq):
    slots = (handover[:N_HANDOVER], handover[N_HANDOVER:])
    s = pl.program_id(0)
    pad = V7X_SUBLANES

    @pl.when(s == 0)
    def _():
        for ref in slots[1]:
            ref[...] = jnp.zeros_like(ref)

    @pl.when(lax.rem(s, tiles_per_seq) == 0)
    def _():
        xpad_ref[:, 0:pad, :] = jnp.zeros((xpad_ref.shape[0], pad, xpad_ref.shape[2]), F32)

    @pl.when((s == 0) | (lax.rem(s - 1, tiles_per_seq) == 0))
    def _():
        state_ref[...] = jnp.zeros_like(state_ref)

    def step(write_slot):
        _interleave(
            _mamba_stage_a(h_ref, g_ref, wz_ref, wx_ref, wdt_ref, wdtT_ref, cw_ref, cb_ref,
                           dtb_ref, dtbT_ref, alog_ref, alogT_ref, xpad_ref, slots[write_slot],
                           n_heads=n_heads),
            _mamba_stage_b(slots[1 - write_slot], dskip_ref, gn_ref, wo_ref, o_ref, state_ref, y_ref,
                           n_heads=n_heads))

    parity = lax.rem(s, 2)

    @pl.when(parity == 0)
    def _():
        step(0)

    @pl.when(parity == 1)
    def _():
        step(1)


def _mamba_block(h3, norm_g, w_in, conv_w, conv_b, dt_bias, a_log, d_skip, gate_norm, w_out):
    bsz, t, d = h3.shape
    n_heads = dt_bias.shape[0]
    d_inner = n_heads * SSM_HEAD_DIM
    conv_dim = d_inner + 2 * SSM_GROUPS * SSM_STATE
    pw = 2 * SSM_HEAD_DIM
    assert w_in.shape == (d, d_inner + conv_dim + n_heads)
    assert n_heads % (2 * SSM_GROUPS) == 0 and n_heads <= V7X_LANES
    assert pw == V7X_LANES and SSM_STATE == V7X_LANES and (conv_dim // pw) % 2 == 0
    L = SSM_TOKEN_TILE
    assert t % L == 0 and L % SSM_SUBCHUNK == 0
    hpg = n_heads // SSM_GROUPS
    tiles_per_seq = t // L
    n_tiles = bsz * tiles_per_seq

    w_in16 = w_in.astype(BF16)
    wz = w_in16[:, :d_inner]
    wx = w_in16[:, d_inner:d_inner + conv_dim]
    wdt = w_in16[:, d_inner + conv_dim:]
    wdt_pad = jnp.pad(wdt, ((0, 0), (0, V7X_LANES - n_heads)))
    wdtT = wdt.T
    dskip_row = jnp.repeat(d_skip.astype(F32), SSM_HEAD_DIM).reshape(1, d_inner)

    arrays = [
        norm_g.reshape(1, d),
        wz, wx, wdt_pad, wdtT,
        conv_w.astype(F32), conv_b.reshape(1, conv_dim).astype(F32),
        dt_bias.reshape(1, n_heads).astype(F32),
        dt_bias.reshape(n_heads, 1).astype(F32),
        a_log.reshape(1, n_heads).astype(F32),
        a_log.reshape(n_heads, 1).astype(F32),
        dskip_row,
        gate_norm.reshape(1, d_inner).astype(F32),
        w_out.astype(BF16),
    ]
    handover = [
        pltpu.VMEM((L, d), F32),
        pltpu.VMEM((L, d_inner), F32),
        pltpu.VMEM((d_inner // pw, L, pw), F32),
        pltpu.VMEM((SSM_GROUPS, L, SSM_STATE), BF16),
        pltpu.VMEM((SSM_GROUPS, L, SSM_STATE), BF16),
        pltpu.VMEM((L, n_heads), F32),
        pltpu.VMEM((L, n_heads), F32),
        pltpu.VMEM((n_heads, L), F32),
    ]
    assert len(handover) == N_HANDOVER
    in_specs = [pl.BlockSpec((L, d), lambda s: (jnp.minimum(s, n_tiles - 1), 0))]
    in_specs += [_const_spec(a.shape) for a in arrays]
    out = pl.pallas_call(
        functools.partial(_mamba_kernel, n_heads=n_heads, tiles_per_seq=tiles_per_seq),
        out_shape=jax.ShapeDtypeStruct((bsz * t, d), F32),
        grid=(n_tiles + 1,),
        in_specs=in_specs,
        out_specs=pl.BlockSpec((L, d), lambda s: (jnp.maximum(s - 1, 0), 0)),
        scratch_shapes=[
            pltpu.VMEM((conv_dim // pw, L + V7X_SUBLANES, pw), F32),
            pltpu.VMEM((SSM_GROUPS, SSM_STATE, hpg * SSM_HEAD_DIM), F32),
            pltpu.VMEM((L, d_inner), F32),
        ] + handover + handover,
        compiler_params=pltpu.CompilerParams(
            dimension_semantics=("arbitrary",),
            vmem_limit_bytes=V7X_VMEM_LIMIT_BYTES),
        name="mamba2_block",
    )(h3.reshape(bsz * t, d), *arrays)
    return out.reshape(bsz, t, d)


def _t5_bucket_table(blk):
    qi = jnp.arange(blk)[:, None] + blk
    kj = jnp.arange(2 * blk)[None, :]
    dist = qi - kj
    n = jnp.maximum(dist, 0)
    max_exact = REL_BUCKETS // 2
    nf = jnp.maximum(n, 1).astype(F32)
    large = max_exact + (jnp.log(nf / max_exact) / math.log(ATT_WINDOW / max_exact)
                         * (REL_BUCKETS - max_exact)).astype(jnp.int32)
    large = jnp.minimum(large, REL_BUCKETS - 1)
    bucket = jnp.where(n < max_exact, n, large)
    in_window = (dist >= 0) & (dist < ATT_WINDOW)
    return jnp.where(in_window, bucket, -1).astype(jnp.int32)


def _attn_kernel(relb_ref, sink_ref, h_ref, g_ref, wq_ref, qn_ref, kp_ref, kc_ref, vp_ref, vc_ref,
                 bucket_ref, wo_ref, o_ref, bias_ref, *, n_heads):
    tq = h_ref.shape[0]
    blk = ATT_WINDOW
    hd = ATT_HEAD_DIM
    pw = 2 * hd
    group = n_heads // ATT_KV_HEADS
    pairs = group // 2
    first = (pl.program_id(0) == 0) & (pl.program_id(1) == 0)

    @pl.when(first)
    def _():
        bias_ref[...] = jnp.full(bias_ref.shape, -jnp.inf, F32)
        bucket = bucket_ref[...]

        def fill(b, carry):
            hit = bucket == b
            for hh in range(n_heads):
                bias_ref[hh] = jnp.where(hit, relb_ref[b, hh] * LOG2E, bias_ref[hh])
            return carry

        lax.fori_loop(0, REL_BUCKETS, fill, 0)

    h = h_ref[...]
    u = _rms_scale(h, g_ref[...]).astype(BF16)
    q = jnp.dot(u, wq_ref[...].astype(BF16), preferred_element_type=F32)
    lane_lo = _pair_lane_mask()

    q16 = jnp.concatenate(
        [(q[:, p * pw:(p + 1) * pw]
          * _pair_rms_scale(q[:, p * pw:(p + 1) * pw], lane_lo, hd ** -0.5 * LOG2E)).astype(BF16)
         for p in range(n_heads // 2)], axis=1)

    kall = jnp.concatenate([kp_ref[...], kc_ref[...]], axis=0).astype(F32) * qn_ref[...]
    vall = jnp.concatenate([vp_ref[...], vc_ref[...]], axis=0).astype(F32)
    kswap = pltpu.roll(kall, hd, 1)
    vswap = pltpu.roll(vall, hd, 1)
    k_slots = []
    v_slots = []
    for kvh in range(ATT_KV_HEADS):
        src_k, alt_k = (kall, kswap) if kvh == 0 else (kswap, kall)
        src_v, alt_v = (vall, vswap) if kvh == 0 else (vswap, vall)
        k_slots.append((jnp.where(lane_lo, src_k, 0.0).astype(BF16), jnp.where(lane_lo, 0.0, alt_k).astype(BF16)))
        v_slots.append((jnp.where(lane_lo, src_v, 0.0).astype(BF16), jnp.where(lane_lo, 0.0, alt_v).astype(BF16)))

    kj = lax.broadcasted_iota(jnp.int32, (1, 2 * blk), 1)
    key_ok = kj >= jnp.where(pl.program_id(1) > 0, 0, blk)

    o_rows = []
    for r in range(tq // blk):
        qrows = slice(r * blk, (r + 1) * blk)
        krows = slice(r * blk, r * blk + 2 * blk)
        out_pairs = []
        for kvh in range(ATT_KV_HEADS):
            q_stack = jnp.concatenate(
                [q16[qrows, (kvh * pairs + j) * pw:(kvh * pairs + j + 1) * pw] for j in range(pairs)],
                axis=0)
            s_slots = [lax.dot_general(q_stack, ks[krows, :], (((1,), (1,)), ((), ())),
                                       preferred_element_type=F32) for ks in k_slots[kvh]]
            v_stack = jnp.concatenate([vs[krows, :] for vs in v_slots[kvh]], axis=0)
            for j in range(pairs):
                probs = []
                rinv = []
                for e in range(2):
                    hh = kvh * group + 2 * j + e
                    s = s_slots[e][j * blk:(j + 1) * blk, :] + bias_ref[hh]
                    if r == 0:
                        s = jnp.where(key_ok, s, -jnp.inf)
                    sink = sink_ref[hh] * LOG2E
                    m = jnp.maximum(jnp.max(s, axis=-1, keepdims=True), sink)
                    pexp = jnp.exp2(s - m)
                    denom = jnp.sum(pexp, axis=-1, keepdims=True) + jnp.exp2(sink - m)
                    probs.append(pexp.astype(BF16))
                    rinv.append(1.0 / denom)
                pv = jnp.dot(jnp.concatenate(probs, axis=1), v_stack, preferred_element_type=F32)
                out_pairs.append((pv * jnp.where(lane_lo, rinv[0], rinv[1])).astype(BF16))
        o_rows.append(jnp.concatenate(out_pairs, axis=1))
    o = jnp.concatenate(o_rows, axis=0)
    o_ref[...] = h + jnp.dot(o, wo_ref[...].astype(BF16), preferred_element_type=F32)


def _attention_block(h3, k, v, norm_g, w_q, q_norm, sinks, rel_bias, w_o):
    bsz, t, d = h3.shape
    n_heads = sinks.shape[0]
    hd = ATT_HEAD_DIM
    kw = ATT_KV_HEADS * hd
    blk = ATT_WINDOW
    assert w_q.shape == (d, n_heads * hd) and w_o.shape == (n_heads * hd, d)
    assert n_heads % (2 * ATT_KV_HEADS) == 0 and kw == V7X_LANES
    tq = min(ATT_TOKEN_TILE, t)
    assert t % tq == 0 and tq % blk == 0
    ratio = tq // blk
    bucket = _t5_bucket_table(blk)
    qn = jnp.tile(q_norm.astype(F32), ATT_KV_HEADS).reshape(1, kw)
    k3 = k.reshape(bsz, t, kw)
    v3 = v.reshape(bsz, t, kw)

    def prev_blk(b, i, *_):
        return (b, jnp.maximum(i * ratio - 1, 0), 0)

    def cur_blk(b, i, *_):
        return (b, i, 0)

    def const2(b, i, *_):
        return (0, 0)

    one = pl.Buffered(1)
    grid_spec = pltpu.PrefetchScalarGridSpec(
        num_scalar_prefetch=2,
        grid=(bsz, t // tq),
        in_specs=[
            pl.BlockSpec((None, tq, d), cur_blk),
            pl.BlockSpec((1, d), const2, pipeline_mode=one),
            pl.BlockSpec((d, n_heads * hd), const2, pipeline_mode=one),
            pl.BlockSpec((1, kw), const2, pipeline_mode=one),
            pl.BlockSpec((None, blk, kw), prev_blk),
            pl.BlockSpec((None, tq, kw), cur_blk),
            pl.BlockSpec((None, blk, kw), prev_blk),
            pl.BlockSpec((None, tq, kw), cur_blk),
            pl.BlockSpec((blk, 2 * blk), const2, pipeline_mode=one),
            pl.BlockSpec((n_heads * hd, d), const2, pipeline_mode=one),
        ],
        out_specs=pl.BlockSpec((None, tq, d), cur_blk),
        scratch_shapes=[pltpu.VMEM((n_heads, blk, 2 * blk), F32)],
    )
    return pl.pallas_call(
        functools.partial(_attn_kernel, n_heads=n_heads),
        out_shape=jax.ShapeDtypeStruct((bsz, t, d), F32),
        grid_spec=grid_spec,
        compiler_params=pltpu.CompilerParams(
            dimension_semantics=("arbitrary", "arbitrary"),
            vmem_limit_bytes=V7X_VMEM_LIMIT_BYTES),
        name="swa_block",
    )(rel_bias.astype(F32), sinks.astype(F32), h3, norm_g.reshape(1, d), w_q.astype(F32), qn,
      k3, k3, v3, v3, bucket, w_o.astype(F32))


def kernel(x, ffn_norm, ffn_w1, ffn_w3, ffn_w2, ssm_norm, ssm_w_in, ssm_conv_w, ssm_conv_b, ssm_dt_bias, ssm_a_log, ssm_d, ssm_gate_norm, ssm_w_out, kv_norm, w_kv, k_norm, attn_norm, w_q, q_norm, sinks, w_o, rel_bias):
    bsz, t, d = x.shape
    depth = ffn_norm.shape[0]
    n_a = ssm_norm.shape[0]
    n = bsz * t
    w1 = ffn_w1.astype(F32)
    w3 = ffn_w3.astype(F32)
    w2 = ffn_w2.astype(F32)

    h = x
    k_shared = v_shared = None
    for layer in range(depth):
        h = _ffn_half_step(h.reshape(n, d), ffn_norm[layer, 0], w1, w3, w2, layer, 0).reshape(bsz, t, d)
        if layer < n_a:
            i = layer
            h = _mamba_block(h, ssm_norm[i], ssm_w_in[i], ssm_conv_w[i], ssm_conv_b[i], ssm_dt_bias[i],
                             ssm_a_log[i], ssm_d[i], ssm_gate_norm[i], ssm_w_out[i])
        else:
            j = layer - n_a
            h = _attention_block(h, k_shared, v_shared, attn_norm[j], w_q[j], q_norm[j], sinks[j],
                                 rel_bias, w_o[j])
        if layer == n_a - 1:
            h2, k2, v2 = _ffn_half_step(h.reshape(n, d), ffn_norm[layer, 1], w1, w3, w2, layer, 1,
                                        kv_params=(kv_norm, w_kv.astype(BF16), k_norm))
            k_shared, v_shared = k2, v2
        else:
            h2 = _ffn_half_step(h.reshape(n, d), ffn_norm[layer, 1], w1, w3, w2, layer, 1)
        h = h2.reshape(bsz, t, d)
    return h
```

```python
import functools
import math

import jax
import jax.numpy as jnp
from jax import lax
from jax.experimental import pallas as pl
from jax.experimental.pallas import tpu as pltpu

F32 = jnp.float32
BF16 = jnp.bfloat16

EPS = 1e-6
FFN_HALF = 0.5
LOG2E = 1.4426950408889634

SSM_HEAD_DIM = 64
SSM_GROUPS = 4
SSM_STATE = 128
SSM_CONV = 4
ATT_HEAD_DIM = 64
ATT_KV_HEADS = 2
ATT_WINDOW = 128
REL_BUCKETS = 32

V7X_LANES = 128
V7X_SUBLANES = 8
V7X_VMEM_LIMIT_BYTES = 56 * 1024 * 1024

FFN_TOKEN_TILE = 1024
FFN_CHUNK = 256
SSM_TOKEN_TILE = 256
SSM_SUBCHUNK = 128
ATT_TOKEN_TILE = 512


def _rms_scale(x, g):
    ms = jnp.mean(x * x, axis=-1, keepdims=True)
    return x * lax.rsqrt(ms + EPS) * g


def _sigmoid(x):
    return 0.5 * jnp.tanh(0.5 * x) + 0.5


def _silu(x):
    return x * _sigmoid(x)


def _const_spec(shape):
    zeros = (0,) * len(shape)
    return pl.BlockSpec(shape, lambda *_: zeros, pipeline_mode=pl.Buffered(1))


def _pair_lane_mask():
    return lax.broadcasted_iota(jnp.int32, (1, 2 * ATT_HEAD_DIM), 1) < ATT_HEAD_DIM


def _pair_rms_scale(x, lane_lo, post):
    xsq = x * x
    s_lo = jnp.sum(jnp.where(lane_lo, xsq, 0.0), axis=-1, keepdims=True)
    s_hi = jnp.sum(jnp.where(lane_lo, 0.0, xsq), axis=-1, keepdims=True)
    inv = 1.0 / ATT_HEAD_DIM
    return jnp.where(lane_lo, lax.rsqrt(s_lo * inv + EPS) * post, lax.rsqrt(s_hi * inv + EPS) * post)


def _ffn_kernel(*refs, n_chunks, fc, with_kv):
    if with_kv:
        h_ref, g_ref, w1_ref, w3_ref, w2_ref, kvg_ref, wkv_ref, kn_ref, o_ref, k_ref, v_ref = refs
    else:
        h_ref, g_ref, w1_ref, w3_ref, w2_ref, o_ref = refs
    h = h_ref[...]
    u = _rms_scale(h, g_ref[...]).astype(BF16)
    for c in range(n_chunks):
        cols = slice(c * fc, (c + 1) * fc)
        a = jnp.dot(u, w1_ref[:, cols].astype(BF16), preferred_element_type=F32)
        b = jnp.dot(u, w3_ref[:, cols].astype(BF16), preferred_element_type=F32)
        gate = (_silu(a) * b).astype(BF16)
        d = jnp.dot(gate, w2_ref[cols, :].astype(BF16), preferred_element_type=F32)
        if c == 0:
            o_ref[...] = d
        else:
            o_ref[...] += d
    out = h + FFN_HALF * o_ref[...]
    o_ref[...] = out
    if with_kv:
        kw = k_ref.shape[1]
        u2 = _rms_scale(out, kvg_ref[...]).astype(BF16)
        kv = jnp.dot(u2, wkv_ref[...], preferred_element_type=F32)
        k = kv[:, :kw]
        k_ref[...] = (k * _pair_rms_scale(k, _pair_lane_mask(), 1.0) * kn_ref[...]).astype(k_ref.dtype)
        v_ref[...] = kv[:, kw:].astype(v_ref.dtype)


def _ffn_half_step(h, g, w1, w3, w2, layer, half, kv_params=None):
    n, d = h.shape
    f = w1.shape[-1]
    tm = min(FFN_TOKEN_TILE, n)
    fc = FFN_CHUNK
    assert n % tm == 0 and f % fc == 0
    with_kv = kv_params is not None

    def wsel(*_):
        return (layer, half, 0, 0)

    one = pl.Buffered(1)
    row = pl.BlockSpec((tm, d), lambda i: (i, 0))
    in_specs = [
        row,
        _const_spec((1, d)),
        pl.BlockSpec((None, None, d, f), wsel, pipeline_mode=one),
        pl.BlockSpec((None, None, d, f), wsel, pipeline_mode=one),
        pl.BlockSpec((None, None, f, d), wsel, pipeline_mode=one),
    ]
    args = [h, g.reshape(1, d), w1, w3, w2]
    out_shape = jax.ShapeDtypeStruct((n, d), F32)
    out_specs = row
    if with_kv:
        kv_norm, w_kv16, k_norm = kv_params
        kw = ATT_KV_HEADS * ATT_HEAD_DIM
        assert w_kv16.shape == (d, 2 * kw) and kw == V7X_LANES
        kn = jnp.tile(k_norm.astype(F32), ATT_KV_HEADS).reshape(1, kw)
        in_specs += [_const_spec((1, d)), _const_spec((d, 2 * kw)), _const_spec((1, kw))]
        args += [kv_norm.reshape(1, d), w_kv16, kn]
        kv_spec = pl.BlockSpec((tm, kw), lambda i: (i, 0))
        out_shape = (out_shape, jax.ShapeDtypeStruct((n, kw), BF16), jax.ShapeDtypeStruct((n, kw), BF16))
        out_specs = (row, kv_spec, kv_spec)
    return pl.pallas_call(
        functools.partial(_ffn_kernel, n_chunks=f // fc, fc=fc, with_kv=with_kv),
        out_shape=out_shape,
        grid=(n // tm,),
        in_specs=in_specs,
        out_specs=out_specs,
        compiler_params=pltpu.CompilerParams(
            dimension_semantics=("arbitrary",), vmem_limit_bytes=V7X_VMEM_LIMIT_BYTES),
        name="ffn_kv_half_step" if with_kv else "ffn_half_step",
    )(*args)


N_HANDOVER = 8


def _pair_cols(v, r0, lane_lo):
    if v.shape[0] == 1:
        return jnp.where(lane_lo, v[:, r0:r0 + 1], v[:, r0 + 1:r0 + 2])
    idx = jnp.where(jnp.broadcast_to(lane_lo, v.shape), r0, r0 + 1)
    return jnp.take_along_axis(v, idx, axis=1)


def _mamba_stage_a(h_ref, g_ref, win_ref, wdt_ref, wdtT_ref, cw_ref, cb_ref,
                   dtb_ref, dtbT_ref, alog_ref, alogT_ref, xpad_ref, bufs, *, n_heads):
    hs_ref, z_ref, xs_ref, b_ref, c_ref, dt_ref, da_ref, daT_ref = bufs
    L = h_ref.shape[0]
    d_inner = z_ref.shape[1]
    pw = 2 * SSM_HEAD_DIM
    n_slab = xpad_ref.shape[0]
    n_xs = xs_ref.shape[0]
    pad = V7X_SUBLANES
    zc = 4 * pw

    h = h_ref[...]
    hs_ref[...] = h
    u = _rms_scale(h, g_ref[...]).astype(BF16)
    yield

    def softplus(x):
        return jnp.maximum(x, 0.0) + jnp.log1p(jnp.exp(-jnp.abs(x)))

    x0 = d_inner
    dt_raw = jnp.dot(u, wdt_ref[...], preferred_element_type=F32)
    dt_rawT = lax.dot_general(wdtT_ref[...], u, (((1,), (1,)), ((), ())),
                              preferred_element_type=F32)
    dt = softplus(dt_raw + dtb_ref[...])
    dtT = softplus(dt_rawT + dtbT_ref[...])
    dt_ref[...] = dt
    da_ref[...] = dt * (-jnp.exp(alog_ref[...]))
    daT_ref[...] = dtT * (-jnp.exp(alogT_ref[...]))
    yield

    for j0 in range(0, n_slab, 2):
        xj = jnp.dot(u, win_ref[:, x0 + j0 * pw:x0 + (j0 + 2) * pw], preferred_element_type=F32)
        xpad_ref[j0, pad:pad + L, :] = xj[:, :pw]
        xpad_ref[j0 + 1, pad:pad + L, :] = xj[:, pw:]
        for j in (j0, j0 + 1):
            acc = cb_ref[:, j * pw:(j + 1) * pw]
            for k in range(SSM_CONV):
                off = pad - (SSM_CONV - 1) + k
                acc = acc + cw_ref[k:k + 1, j * pw:(j + 1) * pw] * xpad_ref[j, off:off + L, :]
            xpad_ref[j, 0:pad, :] = xpad_ref[j, L:L + pad, :]
            act = _silu(acc)
            if j < n_xs:
                xs_ref[j] = act
            elif j < n_xs + SSM_GROUPS:
                b_ref[j - n_xs] = act.astype(BF16)
            else:
                c_ref[j - n_xs - SSM_GROUPS] = act.astype(BF16)
        yield

    for c0 in range(0, d_inner, zc):
        z_ref[:, c0:c0 + zc] = jnp.dot(u, win_ref[:, c0:c0 + zc], preferred_element_type=F32)
        yield


def _mamba_stage_b(bufs, dskip_ref, gn_ref, wo_ref, o_ref, state_ref, y_ref, *, n_heads):
    hs_ref, z_ref, xs_ref, b_ref, c_ref, dt_ref, da_ref, daT_ref = bufs
    L = hs_ref.shape[0]
    ls = SSM_SUBCHUNK
    d_inner = z_ref.shape[1]
    hp = SSM_HEAD_DIM
    pw = 2 * hp
    pairs_per_group = n_heads // SSM_GROUPS // 2

    ri = lax.broadcasted_iota(jnp.int32, (ls, ls), 0)
    ci = lax.broadcasted_iota(jnp.int32, (ls, ls), 1)
    causal = ri >= ci
    tri = jnp.where(causal, 1.0, 0.0).astype(F32)
    triT = jnp.where(ci >= ri, 1.0, 0.0).astype(F32)
    lane_lo = lax.broadcasted_iota(jnp.int32, (1, pw), 1) < hp
    pending = []

    def flush():
        while pending:
            lhs, rhs_, off, rows_, p_ = pending.pop()
            y_ref[rows_, p_ * pw:(p_ + 1) * pw] = jnp.dot(lhs, rhs_, preferred_element_type=F32) + off

    for sc in range(L // ls):
        rows = slice(sc * ls, (sc + 1) * ls)
        acum = jnp.dot(tri, da_ref[rows, :], precision=lax.Precision.HIGHEST,
                       preferred_element_type=F32)
        acumT = jnp.dot(daT_ref[:, rows], triT, precision=lax.Precision.HIGHEST,
                        preferred_element_type=F32)
        a_last = acum[ls - 1:ls, :]
        e_acum = jnp.exp(acum)
        e_last = jnp.exp(a_last)
        dt_s = dt_ref[rows, :]
        w_end = jnp.exp(a_last - acum) * dt_s
        acum2 = acum * LOG2E
        acumT2 = acumT * LOG2E
        yield
        for g in range(SSM_GROUPS):
            cg = c_ref[g, rows, :]
            bg = b_ref[g, rows, :]
            cb = lax.dot_general(cg, bg, (((1,), (1,)), ((), ())),
                                 preferred_element_type=F32)
            st = state_ref[g]
            y_off = jnp.dot(cg, st.astype(BF16), preferred_element_type=F32)
            xw_parts = []
            el_parts = []
            for j in range(pairs_per_group):
                p = g * pairs_per_group + j
                r0 = 2 * p
                xs_pair = xs_ref[p, rows, :]
                xd = xs_pair * _pair_cols(dt_s, r0, lane_lo)
                rhs = jnp.concatenate([jnp.where(lane_lo, xd, 0.0), jnp.where(lane_lo, 0.0, xd)],
                                      axis=0).astype(BF16)
                ms = []
                for r in (r0, r0 + 1):
                    seg = acum2[:, r:r + 1] - acumT2[r:r + 1, :]
                    dec = jnp.exp2(jnp.where(causal, seg, -jnp.inf))
                    ms.append((cb * dec).astype(BF16))
                flush()
                pending.append((jnp.concatenate(ms, axis=1), rhs,
                                y_off[:, j * pw:(j + 1) * pw] * _pair_cols(e_acum, r0, lane_lo), rows, p))
                xw_parts.append((xs_pair * _pair_cols(w_end, r0, lane_lo)).astype(BF16))
                el_parts.append(_pair_cols(e_last, r0, lane_lo))
                if j % 2 == 1 and j + 1 < pairs_per_group:
                    yield
            xw = jnp.concatenate(xw_parts, axis=1)
            el = jnp.concatenate(el_parts, axis=1)
            inc = lax.dot_general(bg, xw, (((0,), (0,)), ((), ())),
                                  preferred_element_type=F32)
            state_ref[g] = st * el + inc
            yield

    flush()
    gw = d_inner // SSM_GROUPS
    slabs_per_group = gw // pw
    for g in range(SSM_GROUPS):
        cols = slice(g * gw, (g + 1) * gw)
        xs_g = jnp.concatenate([xs_ref[g * slabs_per_group + i] for i in range(slabs_per_group)], axis=1)
        yg = (y_ref[:, cols] + dskip_ref[:, cols] * xs_g) * _silu(z_ref[:, cols])
        yn = _rms_scale(yg, gn_ref[:, cols]).astype(BF16)
        part = jnp.dot(yn, wo_ref[cols, :], preferred_element_type=F32)
        if g == 0:
            o_ref[...] = hs_ref[...] + part
        else:
            o_ref[...] += part
        yield


def _interleave(*gens):
    live = list(gens)
    while live:
        for gen in list(live):
            try:
                next(gen)
            except StopIteration:
                live.remove(gen)


def _mamba_kernel(h_ref, g_ref, win_ref, wdt_ref, wdtT_ref, cw_ref, cb_ref,
                  dtb_ref, dtbT_ref, alog_ref, alogT_ref, dskip_ref, gn_ref, wo_ref,
                  o_ref, xpad_ref, state_ref, y_ref, *handover, n_heads, tiles_per_seq):
    slots = (handover[:N_HANDOVER], handover[N_HANDOVER:])
    s = pl.program_id(0)
    pad = V7X_SUBLANES

    @pl.when(s == 0)
    def _():
        for ref in slots[1]:
            ref[...] = jnp.zeros_like(ref)

    @pl.when(lax.rem(s, tiles_per_seq) == 0)
    def _():
        xpad_ref[:, 0:pad, :] = jnp.zeros((xpad_ref.shape[0], pad, xpad_ref.shape[2]), F32)

    @pl.when((s == 0) | (lax.rem(s - 1, tiles_per_seq) == 0))
    def _():
        state_ref[...] = jnp.zeros_like(state_ref)

    def step(write_slot):
        _interleave(
            _mamba_stage_a(h_ref, g_ref, win_ref, wdt_ref, wdtT_ref, cw_ref, cb_ref,
                           dtb_ref, dtbT_ref, alog_ref, alogT_ref, xpad_ref, slots[write_slot],
                           n_heads=n_heads),
            _mamba_stage_b(slots[1 - write_slot], dskip_ref, gn_ref, wo_ref, o_ref, state_ref, y_ref,
                           n_heads=n_heads))

    parity = lax.rem(s, 2)

    @pl.when(parity == 0)
    def _():
        step(0)

    @pl.when(parity == 1)
    def _():
        step(1)


def _mamba_block(h3, norm_g, w_in, conv_w, conv_b, dt_bias, a_log, d_skip, gate_norm, w_out):
    bsz, t, d = h3.shape
    n_heads = dt_bias.shape[0]
    d_inner = n_heads * SSM_HEAD_DIM
    conv_dim = d_inner + 2 * SSM_GROUPS * SSM_STATE
    pw = 2 * SSM_HEAD_DIM
    assert w_in.shape == (d, d_inner + conv_dim + n_heads)
    assert n_heads % (2 * SSM_GROUPS) == 0 and n_heads <= V7X_LANES
    assert pw == V7X_LANES and SSM_STATE == V7X_LANES and (conv_dim // pw) % 2 == 0
    L = SSM_TOKEN_TILE
    assert t % L == 0 and L % SSM_SUBCHUNK == 0
    hpg = n_heads // SSM_GROUPS
    tiles_per_seq = t // L
    n_tiles = bsz * tiles_per_seq

    w_in16 = w_in.astype(BF16)
    wdt = w_in16[:, d_inner + conv_dim:]
    wdtT = wdt.T
    lane_pad = V7X_LANES - n_heads
    wdt_pad = jnp.pad(wdt, ((0, 0), (0, lane_pad)))
    dskip_row = jnp.repeat(d_skip.astype(F32), SSM_HEAD_DIM).reshape(1, d_inner)

    arrays = [
        norm_g.reshape(1, d),
        w_in16, wdt_pad, wdtT,
        conv_w.astype(F32), conv_b.reshape(1, conv_dim).astype(F32),
        jnp.pad(dt_bias.reshape(1, n_heads).astype(F32), ((0, 0), (0, lane_pad))),
        dt_bias.reshape(n_heads, 1).astype(F32),
        jnp.pad(a_log.reshape(1, n_heads).astype(F32), ((0, 0), (0, lane_pad))),
        a_log.reshape(n_heads, 1).astype(F32),
        dskip_row,
        gate_norm.reshape(1, d_inner).astype(F32),
        w_out.astype(BF16),
    ]
    handover = [
        pltpu.VMEM((L, d), F32),
        pltpu.VMEM((L, d_inner), F32),
        pltpu.VMEM((d_inner // pw, L, pw), F32),
        pltpu.VMEM((SSM_GROUPS, L, SSM_STATE), BF16),
        pltpu.VMEM((SSM_GROUPS, L, SSM_STATE), BF16),
        pltpu.VMEM((L, V7X_LANES), F32),
        pltpu.VMEM((L, V7X_LANES), F32),
        pltpu.VMEM((n_heads, L), F32),
    ]
    assert len(handover) == N_HANDOVER
    in_specs = [pl.BlockSpec((L, d), lambda s: (jnp.minimum(s, n_tiles - 1), 0))]
    in_specs += [_const_spec(a.shape) for a in arrays]
    out = pl.pallas_call(
        functools.partial(_mamba_kernel, n_heads=n_heads, tiles_per_seq=tiles_per_seq),
        out_shape=jax.ShapeDtypeStruct((bsz * t, d), F32),
        grid=(n_tiles + 1,),
        in_specs=in_specs,
        out_specs=pl.BlockSpec((L, d), lambda s: (jnp.maximum(s - 1, 0), 0)),
        scratch_shapes=[
            pltpu.VMEM((conv_dim // pw, L + V7X_SUBLANES, pw), F32),
            pltpu.VMEM((SSM_GROUPS, SSM_STATE, hpg * SSM_HEAD_DIM), F32),
            pltpu.VMEM((L, d_inner), F32),
        ] + handover + handover,
        compiler_params=pltpu.CompilerParams(
            dimension_semantics=("arbitrary",),
            vmem_limit_bytes=V7X_VMEM_LIMIT_BYTES),
        name="mamba2_block",
    )(h3.reshape(bsz * t, d), *arrays)
    return out.reshape(bsz, t, d)


def _t5_bucket_table(blk):
    qi = jnp.arange(blk)[:, None] + blk
    kj = jnp.arange(2 * blk)[None, :]
    dist = qi - kj
    n = jnp.maximum(dist, 0)
    max_exact = REL_BUCKETS // 2
    nf = jnp.maximum(n, 1).astype(F32)
    large = max_exact + (jnp.log(nf / max_exact) / math.log(ATT_WINDOW / max_exact)
                         * (REL_BUCKETS - max_exact)).astype(jnp.int32)
    large = jnp.minimum(large, REL_BUCKETS - 1)
    bucket = jnp.where(n < max_exact, n, large)
    in_window = (dist >= 0) & (dist < ATT_WINDOW)
    return jnp.where(in_window, bucket, -1).astype(jnp.int32)


def _attn_kernel(relb_ref, sink_ref, h_ref, g_ref, wq_ref, qn_ref, kp_ref, kc_ref, vp_ref, vc_ref,
                 bucket_ref, wo_ref, o_ref, bias_ref, *, n_heads):
    tq = h_ref.shape[0]
    blk = ATT_WINDOW
    hd = ATT_HEAD_DIM
    pw = 2 * hd
    group = n_heads // ATT_KV_HEADS
    pairs = group // 2
    first = (pl.program_id(0) == 0) & (pl.program_id(1) == 0)

    @pl.when(first)
    def _():
        bias_ref[...] = jnp.full(bias_ref.shape, -jnp.inf, F32)
        bucket = bucket_ref[...]

        def fill(b, carry):
            hit = bucket == b
            for hh in range(n_heads):
                bias_ref[hh] = jnp.where(hit, relb_ref[b, hh] * LOG2E, bias_ref[hh])
            return carry

        lax.fori_loop(0, REL_BUCKETS, fill, 0)

    h = h_ref[...]
    u = _rms_scale(h, g_ref[...]).astype(BF16)
    q = jnp.dot(u, wq_ref[...].astype(BF16), preferred_element_type=F32)
    lane_lo = _pair_lane_mask()

    q16 = jnp.concatenate(
        [(q[:, p * pw:(p + 1) * pw]
          * _pair_rms_scale(q[:, p * pw:(p + 1) * pw], lane_lo, hd ** -0.5 * LOG2E)).astype(BF16)
         for p in range(n_heads // 2)], axis=1)

    kall = jnp.concatenate([kp_ref[...], kc_ref[...]], axis=0).astype(F32) * qn_ref[...]
    vall = jnp.concatenate([vp_ref[...], vc_ref[...]], axis=0).astype(F32)
    kswap = pltpu.roll(kall, hd, 1)
    vswap = pltpu.roll(vall, hd, 1)
    k_slots = []
    v_slots = []
    for kvh in range(ATT_KV_HEADS):
        src_k, alt_k = (kall, kswap) if kvh == 0 else (kswap, kall)
        src_v, alt_v = (vall, vswap) if kvh == 0 else (vswap, vall)
        k_slots.append((jnp.where(lane_lo, src_k, 0.0).astype(BF16), jnp.where(lane_lo, 0.0, alt_k).astype(BF16)))
        v_slots.append((jnp.where(lane_lo, src_v, 0.0).astype(BF16), jnp.where(lane_lo, 0.0, alt_v).astype(BF16)))

    kj = lax.broadcasted_iota(jnp.int32, (1, 2 * blk), 1)
    key_ok = kj >= jnp.where(pl.program_id(1) > 0, 0, blk)

    o_rows = []
    for r in range(tq // blk):
        qrows = slice(r * blk, (r + 1) * blk)
        krows = slice(r * blk, r * blk + 2 * blk)
        out_pairs = []
        for kvh in range(ATT_KV_HEADS):
            q_stack = jnp.concatenate(
                [q16[qrows, (kvh * pairs + j) * pw:(kvh * pairs + j + 1) * pw] for j in range(pairs)],
                axis=0)
            s_slots = [lax.dot_general(q_stack, ks[krows, :], (((1,), (1,)), ((), ())),
                                       preferred_element_type=F32) for ks in k_slots[kvh]]
            v_stack = jnp.concatenate([vs[krows, :] for vs in v_slots[kvh]], axis=0)
            for j in range(pairs):
                probs = []
                rinv = []
                for e in range(2):
                    hh = kvh * group + 2 * j + e
                    s = s_slots[e][j * blk:(j + 1) * blk, :] + bias_ref[hh]
                    if r == 0:
                        s = jnp.where(key_ok, s, -jnp.inf)
                    sink = sink_ref[hh] * LOG2E
                    m = jnp.maximum(jnp.max(s, axis=-1, keepdims=True), sink)
                    pexp = jnp.exp2(s - m)
                    denom = jnp.sum(pexp, axis=-1, keepdims=True) + jnp.exp2(sink - m)
                    probs.append(pexp.astype(BF16))
                    rinv.append(1.0 / denom)
                pv = jnp.dot(jnp.concatenate(probs, axis=1), v_stack, preferred_element_type=F32)
                out_pairs.append((pv * jnp.where(lane_lo, rinv[0], rinv[1])).astype(BF16))
        o_rows.append(jnp.concatenate(out_pairs, axis=1))
    o = jnp.concatenate(o_rows, axis=0)
    o_ref[...] = h + jnp.dot(o, wo_ref[...].astype(BF16), preferred_element_type=F32)


def _attention_block(h3, k, v, norm_g, w_q, q_norm, sinks, rel_bias, w_o):
    bsz, t, d = h3.shape
    n_heads = sinks.shape[0]
    hd = ATT_HEAD_DIM
    kw = ATT_KV_HEADS * hd
    blk = ATT_WINDOW
    assert w_q.shape == (d, n_heads * hd) and w_o.shape == (n_heads * hd, d)
    assert n_heads % (2 * ATT_KV_HEADS) == 0 and kw == V7X_LANES
    tq = min(ATT_TOKEN_TILE, t)
    assert t % tq == 0 and tq % blk == 0
    ratio = tq // blk
    bucket = _t5_bucket_table(blk)
    qn = jnp.tile(q_norm.astype(F32), ATT_KV_HEADS).reshape(1, kw)
    k3 = k.reshape(bsz, t, kw)
    v3 = v.reshape(bsz, t, kw)

    def prev_blk(b, i, *_):
        return (b, jnp.maximum(i * ratio - 1, 0), 0)

    def cur_blk(b, i, *_):
        return (b, i, 0)

    def const2(b, i, *_):
        return (0, 0)

    one = pl.Buffered(1)
    grid_spec = pltpu.PrefetchScalarGridSpec(
        num_scalar_prefetch=2,
        grid=(bsz, t // tq),
        in_specs=[
            pl.BlockSpec((None, tq, d), cur_blk),
            pl.BlockSpec((1, d), const2, pipeline_mode=one),
            pl.BlockSpec((d, n_heads * hd), const2, pipeline_mode=one),
            pl.BlockSpec((1, kw), const2, pipeline_mode=one),
            pl.BlockSpec((None, blk, kw), prev_blk),
            pl.BlockSpec((None, tq, kw), cur_blk),
            pl.BlockSpec((None, blk, kw), prev_blk),
            pl.BlockSpec((None, tq, kw), cur_blk),
            pl.BlockSpec((blk, 2 * blk), const2, pipeline_mode=one),
            pl.BlockSpec((n_heads * hd, d), const2, pipeline_mode=one),
        ],
        out_specs=pl.BlockSpec((None, tq, d), cur_blk),
        scratch_shapes=[pltpu.VMEM((n_heads, blk, 2 * blk), F32)],
    )
    return pl.pallas_call(
        functools.partial(_attn_kernel, n_heads=n_heads),
        out_shape=jax.ShapeDtypeStruct((bsz, t, d), F32),
        grid_spec=grid_spec,
        compiler_params=pltpu.CompilerParams(
            dimension_semantics=("arbitrary", "arbitrary"),
            vmem_limit_bytes=V7X_VMEM_LIMIT_BYTES),
        name="swa_block",
    )(rel_bias.astype(F32), sinks.astype(F32), h3, norm_g.reshape(1, d), w_q.astype(F32), qn,
      k3, k3, v3, v3, bucket, w_o.astype(F32))


def kernel(x, ffn_norm, ffn_w1, ffn_w3, ffn_w2, ssm_norm, ssm_w_in, ssm_conv_w, ssm_conv_b, ssm_dt_bias, ssm_a_log, ssm_d, ssm_gate_norm, ssm_w_out, kv_norm, w_kv, k_norm, attn_norm, w_q, q_norm, sinks, w_o, rel_bias):
    bsz, t, d = x.shape
    depth = ffn_norm.shape[0]
    n_a = ssm_norm.shape[0]
    n = bsz * t
    w1 = ffn_w1.astype(F32)
    w3 = ffn_w3.astype(F32)
    w2 = ffn_w2.astype(F32)

    h = x
    k_shared = v_shared = None
    for layer in range(depth):
        h = _ffn_half_step(h.reshape(n, d), ffn_norm[layer, 0], w1, w3, w2, layer, 0).reshape(bsz, t, d)
        if layer < n_a:
            i = layer
            h = _mamba_block(h, ssm_norm[i], ssm_w_in[i], ssm_conv_w[i], ssm_conv_b[i], ssm_dt_bias[i],
                             ssm_a_log[i], ssm_d[i], ssm_gate_norm[i], ssm_w_out[i])
        else:
            j = layer - n_a
            h = _attention_block(h, k_shared, v_shared, attn_norm[j], w_q[j], q_norm[j], sinks[j],
                                 rel_bias, w_o[j])
        if layer == n_a - 1:
            h2, k2, v2 = _ffn_half_step(h.reshape(n, d), ffn_norm[layer, 1], w1, w3, w2, layer, 1,
                                        kv_params=(kv_norm, w_kv.astype(BF16), k_norm))
            k_shared, v_shared = k2, v2
        else:
            h2 = _ffn_half_step(h.reshape(n, d), ffn_norm[layer, 1], w1, w3, w2, layer, 1)
        h = h2.reshape(bsz, t, d)
    return h
```

```python
import functools
import math

import jax
import jax.numpy as jnp
from jax import lax
from jax.experimental import pallas as pl
from jax.experimental.pallas import tpu as pltpu

F32 = jnp.float32
BF16 = jnp.bfloat16

EPS = 1e-6
FFN_HALF = 0.5
LOG2E = 1.4426950408889634

SSM_HEAD_DIM = 64
SSM_GROUPS = 4
SSM_STATE = 128
SSM_CONV = 4
ATT_HEAD_DIM = 64
ATT_KV_HEADS = 2
ATT_WINDOW = 128
REL_BUCKETS = 32

V7X_LANES = 128
V7X_SUBLANES = 8
V7X_VMEM_LIMIT_BYTES = 56 * 1024 * 1024

FFN_TOKEN_TILE = 1024
FFN_CHUNK = 256
SSM_TOKEN_TILE = 256
SSM_SUBCHUNK = 128
ATT_TOKEN_TILE = 512


def _rms_scale(x, g):
    ms = jnp.mean(x * x, axis=-1, keepdims=True)
    return x * lax.rsqrt(ms + EPS) * g


def _sigmoid(x):
    return 0.5 * jnp.tanh(0.5 * x) + 0.5


def _silu(x):
    return x * _sigmoid(x)


def _const_spec(shape):
    zeros = (0,) * len(shape)
    return pl.BlockSpec(shape, lambda *_: zeros, pipeline_mode=pl.Buffered(1))


def _pair_lane_mask():
    return lax.broadcasted_iota(jnp.int32, (1, 2 * ATT_HEAD_DIM), 1) < ATT_HEAD_DIM


def _pair_rms_scale(x, lane_lo, post):
    xsq = x * x
    s_lo = jnp.sum(jnp.where(lane_lo, xsq, 0.0), axis=-1, keepdims=True)
    s_hi = jnp.sum(jnp.where(lane_lo, 0.0, xsq), axis=-1, keepdims=True)
    inv = 1.0 / ATT_HEAD_DIM
    return jnp.where(lane_lo, lax.rsqrt(s_lo * inv + EPS) * post, lax.rsqrt(s_hi * inv + EPS) * post)


def _ffn_kernel(*refs, n_chunks, fc, with_kv):
    if with_kv:
        h_ref, g_ref, w1_ref, w3_ref, w2_ref, kvg_ref, wkv_ref, kn_ref, o_ref, k_ref, v_ref = refs
    else:
        h_ref, g_ref, w1_ref, w3_ref, w2_ref, o_ref = refs
    h = h_ref[...]
    u = _rms_scale(h, g_ref[...]).astype(BF16)
    for c in range(n_chunks):
        cols = slice(c * fc, (c + 1) * fc)
        a = jnp.dot(u, w1_ref[:, cols].astype(BF16), preferred_element_type=F32)
        b = jnp.dot(u, w3_ref[:, cols].astype(BF16), preferred_element_type=F32)
        gate = (_silu(a) * b).astype(BF16)
        d = jnp.dot(gate, w2_ref[cols, :].astype(BF16), preferred_element_type=F32)
        if c == 0:
            o_ref[...] = d
        else:
            o_ref[...] += d
    out = h + FFN_HALF * o_ref[...]
    o_ref[...] = out
    if with_kv:
        kw = k_ref.shape[1]
        u2 = _rms_scale(out, kvg_ref[...]).astype(BF16)
        kv = jnp.dot(u2, wkv_ref[...], preferred_element_type=F32)
        k = kv[:, :kw]
        k_ref[...] = (k * _pair_rms_scale(k, _pair_lane_mask(), 1.0) * kn_ref[...]).astype(k_ref.dtype)
        v_ref[...] = kv[:, kw:].astype(v_ref.dtype)


def _ffn_half_step(h, g, w1, w3, w2, layer, half, kv_params=None):
    n, d = h.shape
    f = w1.shape[-1]
    tm = min(FFN_TOKEN_TILE, n)
    fc = FFN_CHUNK
    assert n % tm == 0 and f % fc == 0
    with_kv = kv_params is not None

    def wsel(*_):
        return (layer, half, 0, 0)

    one = pl.Buffered(1)
    row = pl.BlockSpec((tm, d), lambda i: (i, 0))
    in_specs = [
        row,
        _const_spec((1, d)),
        pl.BlockSpec((None, None, d, f), wsel, pipeline_mode=one),
        pl.BlockSpec((None, None, d, f), wsel, pipeline_mode=one),
        pl.BlockSpec((None, None, f, d), wsel, pipeline_mode=one),
    ]
    args = [h, g.reshape(1, d), w1, w3, w2]
    out_shape = jax.ShapeDtypeStruct((n, d), F32)
    out_specs = row
    if with_kv:
        kv_norm, w_kv16, k_norm = kv_params
        kw = ATT_KV_HEADS * ATT_HEAD_DIM
        assert w_kv16.shape == (d, 2 * kw) and kw == V7X_LANES
        kn = jnp.tile(k_norm.astype(F32), ATT_KV_HEADS).reshape(1, kw)
        in_specs += [_const_spec((1, d)), _const_spec((d, 2 * kw)), _const_spec((1, kw))]
        args += [kv_norm.reshape(1, d), w_kv16, kn]
        kv_spec = pl.BlockSpec((tm, kw), lambda i: (i, 0))
        out_shape = (out_shape, jax.ShapeDtypeStruct((n, kw), BF16), jax.ShapeDtypeStruct((n, kw), BF16))
        out_specs = (row, kv_spec, kv_spec)
    return pl.pallas_call(
        functools.partial(_ffn_kernel, n_chunks=f // fc, fc=fc, with_kv=with_kv),
        out_shape=out_shape,
        grid=(n // tm,),
        in_specs=in_specs,
        out_specs=out_specs,
        compiler_params=pltpu.CompilerParams(
            dimension_semantics=("arbitrary",), vmem_limit_bytes=V7X_VMEM_LIMIT_BYTES),
        name="ffn_kv_half_step" if with_kv else "ffn_half_step",
    )(*args)


N_HANDOVER = 8


def _pair_cols(v, r0, lane_lo):
    if v.shape[0] == 1:
        return jnp.where(lane_lo, v[:, r0:r0 + 1], v[:, r0 + 1:r0 + 2])
    idx = jnp.where(jnp.broadcast_to(lane_lo, v.shape), r0, r0 + 1)
    return jnp.take_along_axis(v, idx, axis=1)


def _mamba_stage_a(h_ref, g_ref, win_ref, wdt_ref, wdtT_ref, cw_ref, cb_ref,
                   dtb_ref, dtbT_ref, alog_ref, alogT_ref, xpad_ref, bufs, *, n_heads):
    hs_ref, z_ref, xs_ref, b_ref, c_ref, dt_ref, da_ref, daT_ref = bufs
    L = h_ref.shape[0]
    d_inner = z_ref.shape[1]
    pw = 2 * SSM_HEAD_DIM
    n_slab = xpad_ref.shape[0]
    n_xs = xs_ref.shape[0]
    pad = V7X_SUBLANES
    zc = 4 * pw

    h = h_ref[...]
    hs_ref[...] = h
    u = _rms_scale(h, g_ref[...]).astype(BF16)
    yield

    def softplus(x):
        return jnp.maximum(x, 0.0) + jnp.log1p(jnp.exp(-jnp.abs(x)))

    x0 = d_inner
    dt_raw = jnp.dot(u, wdt_ref[...], preferred_element_type=F32)
    dt_rawT = lax.dot_general(wdtT_ref[...], u, (((1,), (1,)), ((), ())),
                              preferred_element_type=F32)
    dt = softplus(dt_raw + dtb_ref[...])
    dtT = softplus(dt_rawT + dtbT_ref[...])
    dt_ref[...] = dt
    da_ref[...] = dt * (-jnp.exp(alog_ref[...]))
    daT_ref[...] = dtT * (-jnp.exp(alogT_ref[...]))
    yield

    for j0 in range(0, n_slab, 2):
        xj = jnp.dot(u, win_ref[:, x0 + j0 * pw:x0 + (j0 + 2) * pw], preferred_element_type=F32)
        xpad_ref[j0, pad:pad + L, :] = xj[:, :pw]
        xpad_ref[j0 + 1, pad:pad + L, :] = xj[:, pw:]
        for j in (j0, j0 + 1):
            acc = cb_ref[:, j * pw:(j + 1) * pw]
            for k in range(SSM_CONV):
                off = pad - (SSM_CONV - 1) + k
                acc = acc + cw_ref[k:k + 1, j * pw:(j + 1) * pw] * xpad_ref[j, off:off + L, :]
            xpad_ref[j, 0:pad, :] = xpad_ref[j, L:L + pad, :]
            act = _silu(acc)
            if j < n_xs:
                xs_ref[j] = act
            elif j < n_xs + SSM_GROUPS:
                b_ref[j - n_xs] = act.astype(BF16)
            else:
                c_ref[j - n_xs - SSM_GROUPS] = act.astype(BF16)
        yield

    for c0 in range(0, d_inner, zc):
        z_ref[:, c0:c0 + zc] = jnp.dot(u, win_ref[:, c0:c0 + zc], preferred_element_type=F32)
        yield


def _mamba_stage_b(bufs, dskip_ref, gn_ref, wo_ref, o_ref, state_ref, y_ref, *, n_heads):
    hs_ref, z_ref, xs_ref, b_ref, c_ref, dt_ref, da_ref, daT_ref = bufs
    L = hs_ref.shape[0]
    ls = SSM_SUBCHUNK
    d_inner = z_ref.shape[1]
    hp = SSM_HEAD_DIM
    pw = 2 * hp
    pairs_per_group = n_heads // SSM_GROUPS // 2

    ri = lax.broadcasted_iota(jnp.int32, (ls, ls), 0)
    ci = lax.broadcasted_iota(jnp.int32, (ls, ls), 1)
    causal = ri >= ci
    tri = jnp.where(causal, 1.0, 0.0).astype(F32)
    triT = jnp.where(ci >= ri, 1.0, 0.0).astype(F32)
    lane_lo = lax.broadcasted_iota(jnp.int32, (1, pw), 1) < hp
    pending = []

    def flush(keep=0):
        while len(pending) > keep:
            lhs, rhs_, off, rows_, p_ = pending.pop(0)
            y_ref[rows_, p_ * pw:(p_ + 1) * pw] = jnp.dot(lhs, rhs_, preferred_element_type=F32) + off

    for sc in range(L // ls):
        rows = slice(sc * ls, (sc + 1) * ls)
        acum = jnp.dot(tri, da_ref[rows, :], precision=lax.Precision.HIGHEST,
                       preferred_element_type=F32)
        acumT = jnp.dot(daT_ref[:, rows], triT, precision=lax.Precision.HIGHEST,
                        preferred_element_type=F32)
        a_last = acum[ls - 1:ls, :]
        e_acum = jnp.exp(acum)
        e_last = jnp.exp(a_last)
        dt_s = dt_ref[rows, :]
        w_end = jnp.exp(a_last - acum) * dt_s
        acum2 = acum * LOG2E
        acumT2 = acumT * LOG2E
        yield
        for g in range(SSM_GROUPS):
            cg = c_ref[g, rows, :]
            bg = b_ref[g, rows, :]
            cb = lax.dot_general(cg, bg, (((1,), (1,)), ((), ())),
                                 preferred_element_type=F32)
            st = state_ref[g]
            y_off = jnp.dot(cg, st.astype(BF16), preferred_element_type=F32)
            xw_parts = []
            el_parts = []
            for j in range(pairs_per_group):
                p = g * pairs_per_group + j
                r0 = 2 * p
                xs_pair = xs_ref[p, rows, :]
                xd = xs_pair * _pair_cols(dt_s, r0, lane_lo)
                rhs = jnp.concatenate([jnp.where(lane_lo, xd, 0.0), jnp.where(lane_lo, 0.0, xd)],
                                      axis=0).astype(BF16)
                ms = []
                for r in (r0, r0 + 1):
                    seg = acum2[:, r:r + 1] - acumT2[r:r + 1, :]
                    dec = jnp.exp2(jnp.where(causal, seg, -jnp.inf))
                    ms.append((cb * dec).astype(BF16))
                flush(keep=2)
                pending.append((jnp.concatenate(ms, axis=1), rhs,
                                y_off[:, j * pw:(j + 1) * pw] * _pair_cols(e_acum, r0, lane_lo), rows, p))
                xw_parts.append((xs_pair * _pair_cols(w_end, r0, lane_lo)).astype(BF16))
                el_parts.append(_pair_cols(e_last, r0, lane_lo))
                if j % 2 == 1 and j + 1 < pairs_per_group:
                    yield
            xw = jnp.concatenate(xw_parts, axis=1)
            el = jnp.concatenate(el_parts, axis=1)
            inc = lax.dot_general(bg, xw, (((0,), (0,)), ((), ())),
                                  preferred_element_type=F32)
            state_ref[g] = st * el + inc
            yield

    flush()
    gw = d_inner // SSM_GROUPS
    slabs_per_group = gw // pw
    for g in range(SSM_GROUPS):
        cols = slice(g * gw, (g + 1) * gw)
        xs_g = jnp.concatenate([xs_ref[g * slabs_per_group + i] for i in range(slabs_per_group)], axis=1)
        yg = (y_ref[:, cols] + dskip_ref[:, cols] * xs_g) * _silu(z_ref[:, cols])
        yn = _rms_scale(yg, gn_ref[:, cols]).astype(BF16)
        part = jnp.dot(yn, wo_ref[cols, :], preferred_element_type=F32)
        if g == 0:
            o_ref[...] = hs_ref[...] + part
        else:
            o_ref[...] += part
        yield


def _interleave(*gens):
    live = list(gens)
    while live:
        for gen in list(live):
            try:
                next(gen)
            except StopIteration:
                live.remove(gen)


def _mamba_kernel(h_ref, g_ref, win_ref, wdt_ref, wdtT_ref, cw_ref, cb_ref,
                  dtb_ref, dtbT_ref, alog_ref, alogT_ref, dskip_ref, gn_ref, wo_ref,
                  o_ref, xpad_ref, state_ref, y_ref, *handover, n_heads, tiles_per_seq):
    slots = (handover[:N_HANDOVER], handover[N_HANDOVER:])
    s = pl.program_id(0)
    pad = V7X_SUBLANES

    @pl.when(s == 0)
    def _():
        for ref in slots[1]:
            ref[...] = jnp.zeros_like(ref)

    @pl.when(lax.rem(s, tiles_per_seq) == 0)
    def _():
        xpad_ref[:, 0:pad, :] = jnp.zeros((xpad_ref.shape[0], pad, xpad_ref.shape[2]), F32)

    @pl.when((s == 0) | (lax.rem(s - 1, tiles_per_seq) == 0))
    def _():
        state_ref[...] = jnp.zeros_like(state_ref)

    def step(write_slot):
        _interleave(
            _mamba_stage_a(h_ref, g_ref, win_ref, wdt_ref, wdtT_ref, cw_ref, cb_ref,
                           dtb_ref, dtbT_ref, alog_ref, alogT_ref, xpad_ref, slots[write_slot],
                           n_heads=n_heads),
            _mamba_stage_b(slots[1 - write_slot], dskip_ref, gn_ref, wo_ref, o_ref, state_ref, y_ref,
                           n_heads=n_heads))

    parity = lax.rem(s, 2)

    @pl.when(parity == 0)
    def _():
        step(0)

    @pl.when(parity == 1)
    def _():
        step(1)


def _mamba_block(h3, norm_g, w_in, conv_w, conv_b, dt_bias, a_log, d_skip, gate_norm, w_out):
    bsz, t, d = h3.shape
    n_heads = dt_bias.shape[0]
    d_inner = n_heads * SSM_HEAD_DIM
    conv_dim = d_inner + 2 * SSM_GROUPS * SSM_STATE
    pw = 2 * SSM_HEAD_DIM
    assert w_in.shape == (d, d_inner + conv_dim + n_heads)
    assert n_heads % (2 * SSM_GROUPS) == 0 and n_heads <= V7X_LANES
    assert pw == V7X_LANES and SSM_STATE == V7X_LANES and (conv_dim // pw) % 2 == 0
    L = SSM_TOKEN_TILE
    assert t % L == 0 and L % SSM_SUBCHUNK == 0
    hpg = n_heads // SSM_GROUPS
    tiles_per_seq = t // L
    n_tiles = bsz * tiles_per_seq

    w_in16 = w_in.astype(BF16)
    wdt = w_in16[:, d_inner + conv_dim:]
    wdtT = wdt.T
    lane_pad = V7X_LANES - n_heads
    wdt_pad = jnp.pad(wdt, ((0, 0), (0, lane_pad)))
    dskip_row = jnp.repeat(d_skip.astype(F32), SSM_HEAD_DIM).reshape(1, d_inner)

    arrays = [
        norm_g.reshape(1, d),
        w_in16, wdt_pad, wdtT,
        conv_w.astype(F32), conv_b.reshape(1, conv_dim).astype(F32),
        jnp.pad(dt_bias.reshape(1, n_heads).astype(F32), ((0, 0), (0, lane_pad))),
        dt_bias.reshape(n_heads, 1).astype(F32),
        jnp.pad(a_log.reshape(1, n_heads).astype(F32), ((0, 0), (0, lane_pad))),
        a_log.reshape(n_heads, 1).astype(F32),
        dskip_row,
        gate_norm.reshape(1, d_inner).astype(F32),
        w_out.astype(BF16),
    ]
    handover = [
        pltpu.VMEM((L, d), F32),
        pltpu.VMEM((L, d_inner), F32),
        pltpu.VMEM((d_inner // pw, L, pw), F32),
        pltpu.VMEM((SSM_GROUPS, L, SSM_STATE), BF16),
        pltpu.VMEM((SSM_GROUPS, L, SSM_STATE), BF16),
        pltpu.VMEM((L, V7X_LANES), F32),
        pltpu.VMEM((L, V7X_LANES), F32),
        pltpu.VMEM((n_heads, L), F32),
    ]
    assert len(handover) == N_HANDOVER
    in_specs = [pl.BlockSpec((L, d), lambda s: (jnp.minimum(s, n_tiles - 1), 0))]
    in_specs += [_const_spec(a.shape) for a in arrays]
    out = pl.pallas_call(
        functools.partial(_mamba_kernel, n_heads=n_heads, tiles_per_seq=tiles_per_seq),
        out_shape=jax.ShapeDtypeStruct((bsz * t, d), F32),
        grid=(n_tiles + 1,),
        in_specs=in_specs,
        out_specs=pl.BlockSpec((L, d), lambda s: (jnp.maximum(s - 1, 0), 0)),
        scratch_shapes=[
            pltpu.VMEM((conv_dim // pw, L + V7X_SUBLANES, pw), F32),
            pltpu.VMEM((SSM_GROUPS, SSM_STATE, hpg * SSM_HEAD_DIM), F32),
            pltpu.VMEM((L, d_inner), F32),
        ] + handover + handover,
        compiler_params=pltpu.CompilerParams(
            dimension_semantics=("arbitrary",),
            vmem_limit_bytes=V7X_VMEM_LIMIT_BYTES),
        name="mamba2_block",
    )(h3.reshape(bsz * t, d), *arrays)
    return out.reshape(bsz, t, d)


def _t5_bucket_table(blk):
    qi = jnp.arange(blk)[:, None] + blk
    kj = jnp.arange(2 * blk)[None, :]
    dist = qi - kj
    n = jnp.maximum(dist, 0)
    max_exact = REL_BUCKETS // 2
    nf = jnp.maximum(n, 1).astype(F32)
    large = max_exact + (jnp.log(nf / max_exact) / math.log(ATT_WINDOW / max_exact)
                         * (REL_BUCKETS - max_exact)).astype(jnp.int32)
    large = jnp.minimum(large, REL_BUCKETS - 1)
    bucket = jnp.where(n < max_exact, n, large)
    in_window = (dist >= 0) & (dist < ATT_WINDOW)
    return jnp.where(in_window, bucket, -1).astype(jnp.int32)


def _attn_kernel(relb_ref, sink_ref, h_ref, g_ref, wq_ref, qn_ref, kp_ref, kc_ref, vp_ref, vc_ref,
                 bucket_ref, wo_ref, o_ref, bias_ref, *, n_heads):
    tq = h_ref.shape[0]
    blk = ATT_WINDOW
    hd = ATT_HEAD_DIM
    pw = 2 * hd
    group = n_heads // ATT_KV_HEADS
    pairs = group // 2
    first = (pl.program_id(0) == 0) & (pl.program_id(1) == 0)

    @pl.when(first)
    def _():
        bias_ref[...] = jnp.full(bias_ref.shape, -jnp.inf, F32)
        bucket = bucket_ref[...]

        def fill(b, carry):
            hit = bucket == b
            for hh in range(n_heads):
                bias_ref[hh] = jnp.where(hit, relb_ref[b, hh] * LOG2E, bias_ref[hh])
            return carry

        lax.fori_loop(0, REL_BUCKETS, fill, 0)

    h = h_ref[...]
    u = _rms_scale(h, g_ref[...]).astype(BF16)
    q = jnp.dot(u, wq_ref[...].astype(BF16), preferred_element_type=F32)
    lane_lo = _pair_lane_mask()

    q16 = jnp.concatenate(
        [(q[:, p * pw:(p + 1) * pw]
          * _pair_rms_scale(q[:, p * pw:(p + 1) * pw], lane_lo, hd ** -0.5 * LOG2E)).astype(BF16)
         for p in range(n_heads // 2)], axis=1)

    kall = jnp.concatenate([kp_ref[...], kc_ref[...]], axis=0).astype(F32) * qn_ref[...]
    vall = jnp.concatenate([vp_ref[...], vc_ref[...]], axis=0).astype(F32)
    kswap = pltpu.roll(kall, hd, 1)
    vswap = pltpu.roll(vall, hd, 1)
    k_slots = []
    v_slots = []
    for kvh in range(ATT_KV_HEADS):
        src_k, alt_k = (kall, kswap) if kvh == 0 else (kswap, kall)
        src_v, alt_v = (vall, vswap) if kvh == 0 else (vswap, vall)
        k_slots.append((jnp.where(lane_lo, src_k, 0.0).astype(BF16), jnp.where(lane_lo, 0.0, alt_k).astype(BF16)))
        v_slots.append((jnp.where(lane_lo, src_v, 0.0).astype(BF16), jnp.where(lane_lo, 0.0, alt_v).astype(BF16)))

    kj = lax.broadcasted_iota(jnp.int32, (1, 2 * blk), 1)
    key_ok = kj >= jnp.where(pl.program_id(1) > 0, 0, blk)

    o_rows = []
    for r in range(tq // blk):
        qrows = slice(r * blk, (r + 1) * blk)
        krows = slice(r * blk, r * blk + 2 * blk)
        out_pairs = []
        for kvh in range(ATT_KV_HEADS):
            q_stack = jnp.concatenate(
                [q16[qrows, (kvh * pairs + j) * pw:(kvh * pairs + j + 1) * pw] for j in range(pairs)],
                axis=0)
            s_slots = [lax.dot_general(q_stack, ks[krows, :], (((1,), (1,)), ((), ())),
                                       preferred_element_type=F32) for ks in k_slots[kvh]]
            v_stack = jnp.concatenate([vs[krows, :] for vs in v_slots[kvh]], axis=0)
            p_rows = []
            scales = []
            for j in range(pairs):
                probs = []
                rinv = []
                for e in range(2):
                    hh = kvh * group + 2 * j + e
                    s = s_slots[e][j * blk:(j + 1) * blk, :] + bias_ref[hh]
                    if r == 0:
                        s = jnp.where(key_ok, s, -jnp.inf)
                    sink = sink_ref[hh] * LOG2E
                    m = jnp.maximum(jnp.max(s, axis=-1, keepdims=True), sink)
                    pexp = jnp.exp2(s - m)
                    denom = jnp.sum(pexp, axis=-1, keepdims=True) + jnp.exp2(sink - m)
                    probs.append(pexp.astype(BF16))
                    rinv.append(1.0 / denom)
                p_rows.append(jnp.concatenate(probs, axis=1))
                scales.append(jnp.where(lane_lo, rinv[0], rinv[1]))
            pv = jnp.dot(jnp.concatenate(p_rows, axis=0), v_stack, preferred_element_type=F32)
            for j in range(pairs):
                out_pairs.append((pv[j * blk:(j + 1) * blk, :] * scales[j]).astype(BF16))
        o_rows.append(jnp.concatenate(out_pairs, axis=1))
    o = jnp.concatenate(o_rows, axis=0)
    o_ref[...] = h + jnp.dot(o, wo_ref[...].astype(BF16), preferred_element_type=F32)


def _attention_block(h3, k, v, norm_g, w_q, q_norm, sinks, rel_bias, w_o):
    bsz, t, d = h3.shape
    n_heads = sinks.shape[0]
    hd = ATT_HEAD_DIM
    kw = ATT_KV_HEADS * hd
    blk = ATT_WINDOW
    assert w_q.shape == (d, n_heads * hd) and w_o.shape == (n_heads * hd, d)
    assert n_heads % (2 * ATT_KV_HEADS) == 0 and kw == V7X_LANES
    tq = min(ATT_TOKEN_TILE, t)
    assert t % tq == 0 and tq % blk == 0
    ratio = tq // blk
    bucket = _t5_bucket_table(blk)
    qn = jnp.tile(q_norm.astype(F32), ATT_KV_HEADS).reshape(1, kw)
    k3 = k.reshape(bsz, t, kw)
    v3 = v.reshape(bsz, t, kw)

    def prev_blk(b, i, *_):
        return (b, jnp.maximum(i * ratio - 1, 0), 0)

    def cur_blk(b, i, *_):
        return (b, i, 0)

    def const2(b, i, *_):
        return (0, 0)

    one = pl.Buffered(1)
    grid_spec = pltpu.PrefetchScalarGridSpec(
        num_scalar_prefetch=2,
        grid=(bsz, t // tq),
        in_specs=[
            pl.BlockSpec((None, tq, d), cur_blk),
            pl.BlockSpec((1, d), const2, pipeline_mode=one),
            pl.BlockSpec((d, n_heads * hd), const2, pipeline_mode=one),
            pl.BlockSpec((1, kw), const2, pipeline_mode=one),
            pl.BlockSpec((None, blk, kw), prev_blk),
            pl.BlockSpec((None, tq, kw), cur_blk),
            pl.BlockSpec((None, blk, kw), prev_blk),
            pl.BlockSpec((None, tq, kw), cur_blk),
            pl.BlockSpec((blk, 2 * blk), const2, pipeline_mode=one),
            pl.BlockSpec((n_heads * hd, d), const2, pipeline_mode=one),
        ],
        out_specs=pl.BlockSpec((None, tq, d), cur_blk),
        scratch_shapes=[pltpu.VMEM((n_heads, blk, 2 * blk), F32)],
    )
    return pl.pallas_call(
        functools.partial(_attn_kernel, n_heads=n_heads),
        out_shape=jax.ShapeDtypeStruct((bsz, t, d), F32),
        grid_spec=grid_spec,
        compiler_params=pltpu.CompilerParams(
            dimension_semantics=("arbitrary", "arbitrary"),
            vmem_limit_bytes=V7X_VMEM_LIMIT_BYTES),
        name="swa_block",
    )(rel_bias.astype(F32), sinks.astype(F32), h3, norm_g.reshape(1, d), w_q.astype(F32), qn,
      k3, k3, v3, v3, bucket, w_o.astype(F32))


def kernel(x, ffn_norm, ffn_w1, ffn_w3, ffn_w2, ssm_norm, ssm_w_in, ssm_conv_w, ssm_conv_b, ssm_dt_bias, ssm_a_log, ssm_d, ssm_gate_norm, ssm_w_out, kv_norm, w_kv, k_norm, attn_norm, w_q, q_norm, sinks, w_o, rel_bias):
    bsz, t, d = x.shape
    depth = ffn_norm.shape[0]
    n_a = ssm_norm.shape[0]
    n = bsz * t
    w1 = ffn_w1.astype(F32)
    w3 = ffn_w3.astype(F32)
    w2 = ffn_w2.astype(F32)

    h = x
    k_shared = v_shared = None
    for layer in range(depth):
        h = _ffn_half_step(h.reshape(n, d), ffn_norm[layer, 0], w1, w3, w2, layer, 0).reshape(bsz, t, d)
        if layer < n_a:
            i = layer
            h = _mamba_block(h, ssm_norm[i], ssm_w_in[i], ssm_conv_w[i], ssm_conv_b[i], ssm_dt_bias[i],
                             ssm_a_log[i], ssm_d[i], ssm_gate_norm[i], ssm_w_out[i])
        else:
            j = layer - n_a
            h = _attention_block(h, k_shared, v_shared, attn_norm[j], w_q[j], q_norm[j], sinks[j],
                                 rel_bias, w_o[j])
        if layer == n_a - 1:
            h2, k2, v2 = _ffn_half_step(h.reshape(n, d), ffn_norm[layer, 1], w1, w3, w2, layer, 1,
                                        kv_params=(kv_norm, w_kv.astype(BF16), k_norm))
            k_shared, v_shared = k2, v2
        else:
            h2 = _ffn_half_step(h.reshape(n, d), ffn_norm[layer, 1], w1, w3, w2, layer, 1)
        h = h2.reshape(bsz, t, d)
    return h
```

```python
import functools
import math

import jax
import jax.numpy as jnp
from jax import lax
from jax.experimental import pallas as pl
from jax.experimental.pallas import tpu as pltpu

F32 = jnp.float32
BF16 = jnp.bfloat16

EPS = 1e-6
FFN_HALF = 0.5
LOG2E = 1.4426950408889634

SSM_HEAD_DIM = 64
SSM_GROUPS = 4
SSM_STATE = 128
SSM_CONV = 4
ATT_HEAD_DIM = 64
ATT_KV_HEADS = 2
ATT_WINDOW = 128
REL_BUCKETS = 32

V7X_LANES = 128
V7X_SUBLANES = 8
V7X_VMEM_LIMIT_BYTES = 56 * 1024 * 1024

FFN_TOKEN_TILE = 1024
FFN_CHUNK = 256
SSM_TOKEN_TILE = 256
SSM_SUBCHUNK = 128
ATT_TOKEN_TILE = 512


def _rms_scale(x, g):
    ms = jnp.mean(x * x, axis=-1, keepdims=True)
    return x * lax.rsqrt(ms + EPS) * g


def _sigmoid(x):
    return 0.5 * jnp.tanh(0.5 * x) + 0.5


def _silu(x):
    return x * _sigmoid(x)


def _const_spec(shape):
    zeros = (0,) * len(shape)
    return pl.BlockSpec(shape, lambda *_: zeros, pipeline_mode=pl.Buffered(1))


def _pair_lane_mask():
    return lax.broadcasted_iota(jnp.int32, (1, 2 * ATT_HEAD_DIM), 1) < ATT_HEAD_DIM


def _pair_rms_scale(x, lane_lo, post):
    xsq = x * x
    s_lo = jnp.sum(jnp.where(lane_lo, xsq, 0.0), axis=-1, keepdims=True)
    s_hi = jnp.sum(jnp.where(lane_lo, 0.0, xsq), axis=-1, keepdims=True)
    inv = 1.0 / ATT_HEAD_DIM
    return jnp.where(lane_lo, lax.rsqrt(s_lo * inv + EPS) * post, lax.rsqrt(s_hi * inv + EPS) * post)


def _ffn_kernel(*refs, n_chunks, fc, with_kv):
    if with_kv:
        h_ref, g_ref, w1_ref, w3_ref, w2_ref, kvg_ref, wkv_ref, kn_ref, o_ref, k_ref, v_ref = refs
    else:
        h_ref, g_ref, w1_ref, w3_ref, w2_ref, o_ref = refs
    h = h_ref[...]
    u = _rms_scale(h, g_ref[...]).astype(BF16)
    for c in range(n_chunks):
        cols = slice(c * fc, (c + 1) * fc)
        a = jnp.dot(u, w1_ref[:, cols].astype(BF16), preferred_element_type=F32)
        b = jnp.dot(u, w3_ref[:, cols].astype(BF16), preferred_element_type=F32)
        gate = (_silu(a) * b).astype(BF16)
        d = jnp.dot(gate, w2_ref[cols, :].astype(BF16), preferred_element_type=F32)
        if c == 0:
            o_ref[...] = d
        else:
            o_ref[...] += d
    out = h + FFN_HALF * o_ref[...]
    o_ref[...] = out
    if with_kv:
        kw = k_ref.shape[1]
        u2 = _rms_scale(out, kvg_ref[...]).astype(BF16)
        kv = jnp.dot(u2, wkv_ref[...], preferred_element_type=F32)
        k = kv[:, :kw]
        k_ref[...] = (k * _pair_rms_scale(k, _pair_lane_mask(), 1.0) * kn_ref[...]).astype(k_ref.dtype)
        v_ref[...] = kv[:, kw:].astype(v_ref.dtype)


def _ffn_half_step(h, g, w1, w3, w2, layer, half, kv_params=None):
    n, d = h.shape
    f = w1.shape[-1]
    tm = min(FFN_TOKEN_TILE, n)
    fc = FFN_CHUNK
    assert n % tm == 0 and f % fc == 0
    with_kv = kv_params is not None

    def wsel(*_):
        return (layer, half, 0, 0)

    one = pl.Buffered(1)
    row = pl.BlockSpec((tm, d), lambda i: (i, 0))
    in_specs = [
        row,
        _const_spec((1, d)),
        pl.BlockSpec((None, None, d, f), wsel, pipeline_mode=one),
        pl.BlockSpec((None, None, d, f), wsel, pipeline_mode=one),
        pl.BlockSpec((None, None, f, d), wsel, pipeline_mode=one),
    ]
    args = [h, g.reshape(1, d), w1, w3, w2]
    out_shape = jax.ShapeDtypeStruct((n, d), F32)
    out_specs = row
    if with_kv:
        kv_norm, w_kv16, k_norm = kv_params
        kw = ATT_KV_HEADS * ATT_HEAD_DIM
        assert w_kv16.shape == (d, 2 * kw) and kw == V7X_LANES
        kn = jnp.tile(k_norm.astype(F32), ATT_KV_HEADS).reshape(1, kw)
        in_specs += [_const_spec((1, d)), _const_spec((d, 2 * kw)), _const_spec((1, kw))]
        args += [kv_norm.reshape(1, d), w_kv16, kn]
        kv_spec = pl.BlockSpec((tm, kw), lambda i: (i, 0))
        out_shape = (out_shape, jax.ShapeDtypeStruct((n, kw), BF16), jax.ShapeDtypeStruct((n, kw), BF16))
        out_specs = (row, kv_spec, kv_spec)
    return pl.pallas_call(
        functools.partial(_ffn_kernel, n_chunks=f // fc, fc=fc, with_kv=with_kv),
        out_shape=out_shape,
        grid=(n // tm,),
        in_specs=in_specs,
        out_specs=out_specs,
        compiler_params=pltpu.CompilerParams(
            dimension_semantics=("arbitrary",), vmem_limit_bytes=V7X_VMEM_LIMIT_BYTES),
        name="ffn_kv_half_step" if with_kv else "ffn_half_step",
    )(*args)


N_HANDOVER = 8


def _pair_cols(v, r0, lane_lo):
    if v.shape[0] == 1:
        return jnp.where(lane_lo, v[:, r0:r0 + 1], v[:, r0 + 1:r0 + 2])
    idx = jnp.where(jnp.broadcast_to(lane_lo, v.shape), r0, r0 + 1)
    return jnp.take_along_axis(v, idx, axis=1)


def _bf16_terms(x):
    hi = x.astype(BF16)
    r1 = x - hi.astype(F32)
    mid = r1.astype(BF16)
    lo = (r1 - mid.astype(F32)).astype(BF16)
    return hi, mid, lo


def _exact_f32_dot(a, b, split_rhs):
    if split_rhs:
        n = b.shape[1]
        out = jnp.dot(a, jnp.concatenate(_bf16_terms(b), axis=1), preferred_element_type=F32)
        return out[:, :n] + out[:, n:2 * n] + out[:, 2 * n:]
    m = a.shape[0]
    out = jnp.dot(jnp.concatenate(_bf16_terms(a), axis=0), b, preferred_element_type=F32)
    return out[:m] + out[m:2 * m] + out[2 * m:]


def _mamba_stage_a(h_ref, g_ref, win_ref, wdt_ref, wdtT_ref, cw_ref, cb_ref,
                   dtb_ref, dtbT_ref, alog_ref, alogT_ref, xpad_ref, bufs, *, n_heads):
    hs_ref, z_ref, xs_ref, b_ref, c_ref, dt_ref, da_ref, daT_ref = bufs
    L = h_ref.shape[0]
    d_inner = z_ref.shape[1]
    pw = 2 * SSM_HEAD_DIM
    n_slab = xpad_ref.shape[0]
    n_xs = xs_ref.shape[0]
    pad = V7X_SUBLANES
    zc = 4 * pw

    h = h_ref[...]
    hs_ref[...] = h
    u = _rms_scale(h, g_ref[...]).astype(BF16)
    yield

    def softplus(x):
        return jnp.maximum(x, 0.0) + jnp.log1p(jnp.exp(-jnp.abs(x)))

    x0 = d_inner
    dt_raw = jnp.dot(u, wdt_ref[...], preferred_element_type=F32)
    dt_rawT = lax.dot_general(wdtT_ref[...], u, (((1,), (1,)), ((), ())),
                              preferred_element_type=F32)
    dt = softplus(dt_raw + dtb_ref[...])
    dtT = softplus(dt_rawT + dtbT_ref[...])
    dt_ref[...] = dt
    da_ref[...] = dt * (-jnp.exp(alog_ref[...]))
    daT_ref[...] = dtT * (-jnp.exp(alogT_ref[...]))
    yield

    for j0 in range(0, n_slab, 2):
        xj = jnp.dot(u, win_ref[:, x0 + j0 * pw:x0 + (j0 + 2) * pw], preferred_element_type=F32)
        xpad_ref[j0, pad:pad + L, :] = xj[:, :pw]
        xpad_ref[j0 + 1, pad:pad + L, :] = xj[:, pw:]
        for j in (j0, j0 + 1):
            acc = cb_ref[:, j * pw:(j + 1) * pw]
            for k in range(SSM_CONV):
                off = pad - (SSM_CONV - 1) + k
                acc = acc + cw_ref[k:k + 1, j * pw:(j + 1) * pw] * xpad_ref[j, off:off + L, :]
            xpad_ref[j, 0:pad, :] = xpad_ref[j, L:L + pad, :]
            act = _silu(acc)
            if j < n_xs:
                xs_ref[j] = act
            elif j < n_xs + SSM_GROUPS:
                b_ref[j - n_xs] = act.astype(BF16)
            else:
                c_ref[j - n_xs - SSM_GROUPS] = act.astype(BF16)
        yield

    for c0 in range(0, d_inner, zc):
        z_ref[:, c0:c0 + zc] = jnp.dot(u, win_ref[:, c0:c0 + zc], preferred_element_type=F32)
        yield


def _mamba_stage_b(bufs, dskip_ref, gn_ref, wo_ref, o_ref, state_ref, y_ref, *, n_heads):
    hs_ref, z_ref, xs_ref, b_ref, c_ref, dt_ref, da_ref, daT_ref = bufs
    L = hs_ref.shape[0]
    ls = SSM_SUBCHUNK
    d_inner = z_ref.shape[1]
    hp = SSM_HEAD_DIM
    pw = 2 * hp
    pairs_per_group = n_heads // SSM_GROUPS // 2

    ri = lax.broadcasted_iota(jnp.int32, (ls, ls), 0)
    ci = lax.broadcasted_iota(jnp.int32, (ls, ls), 1)
    causal = ri >= ci
    tri = jnp.where(causal, 1.0, 0.0).astype(BF16)
    triT = jnp.where(ci >= ri, 1.0, 0.0).astype(BF16)
    lane_lo = lax.broadcasted_iota(jnp.int32, (1, pw), 1) < hp
    pending = []
    state_todo = []

    def update_state():
        while state_todo:
            g_, bg_, xw_, el_ = state_todo.pop(0)
            inc = lax.dot_general(bg_, xw_, (((0,), (0,)), ((), ())),
                                  preferred_element_type=F32)
            state_ref[g_] = state_ref[g_] * el_ + inc

    def flush(keep=0):
        while len(pending) > keep:
            lhs, rhs_, off, rows_, p_ = pending.pop(0)
            y_ref[rows_, p_ * pw:(p_ + 1) * pw] = jnp.dot(lhs, rhs_, preferred_element_type=F32) + off

    for sc in range(L // ls):
        rows = slice(sc * ls, (sc + 1) * ls)
        acum = _exact_f32_dot(tri, da_ref[rows, :], split_rhs=True)
        acumT = _exact_f32_dot(daT_ref[:, rows], triT, split_rhs=False)
        a_last = acum[ls - 1:ls, :]
        e_acum = jnp.exp(acum)
        e_last = jnp.exp(a_last)
        dt_s = dt_ref[rows, :]
        w_end = jnp.exp(a_last - acum) * dt_s
        acum2 = acum * LOG2E
        acumT2 = acumT * LOG2E
        yield
        for g in range(SSM_GROUPS):
            cg = c_ref[g, rows, :]
            bg = b_ref[g, rows, :]
            cb = lax.dot_general(cg, bg, (((1,), (1,)), ((), ())),
                                 preferred_element_type=F32)
            st = state_ref[g]
            y_off = jnp.dot(cg, st.astype(BF16), preferred_element_type=F32)
            xw_parts = []
            el_parts = []
            for j in range(pairs_per_group):
                p = g * pairs_per_group + j
                r0 = 2 * p
                xs_pair = xs_ref[p, rows, :]
                xd = xs_pair * _pair_cols(dt_s, r0, lane_lo)
                rhs = jnp.concatenate([jnp.where(lane_lo, xd, 0.0), jnp.where(lane_lo, 0.0, xd)],
                                      axis=0).astype(BF16)
                ms = []
                for r in (r0, r0 + 1):
                    seg = acum2[:, r:r + 1] - acumT2[r:r + 1, :]
                    dec = jnp.exp2(jnp.where(causal, seg, -jnp.inf))
                    ms.append((cb * dec).astype(BF16))
                flush(keep=2)
                pending.append((jnp.concatenate(ms, axis=1), rhs,
                                y_off[:, j * pw:(j + 1) * pw] * _pair_cols(e_acum, r0, lane_lo), rows, p))
                xw_parts.append((xs_pair * _pair_cols(w_end, r0, lane_lo)).astype(BF16))
                el_parts.append(_pair_cols(e_last, r0, lane_lo))
                if j % 2 == 1 and j + 1 < pairs_per_group:
                    yield
            xw = jnp.concatenate(xw_parts, axis=1)
            el = jnp.concatenate(el_parts, axis=1)
            update_state()
            state_todo.append((g, bg, xw, el))
            yield

    flush()
    update_state()
    gw = d_inner // SSM_GROUPS
    slabs_per_group = gw // pw
    for g in range(SSM_GROUPS):
        cols = slice(g * gw, (g + 1) * gw)
        xs_g = jnp.concatenate([xs_ref[g * slabs_per_group + i] for i in range(slabs_per_group)], axis=1)
        yg = (y_ref[:, cols] + dskip_ref[:, cols] * xs_g) * _silu(z_ref[:, cols])
        yn = _rms_scale(yg, gn_ref[:, cols]).astype(BF16)
        part = jnp.dot(yn, wo_ref[cols, :], preferred_element_type=F32)
        if g == 0:
            o_ref[...] = hs_ref[...] + part
        else:
            o_ref[...] += part
        yield


def _interleave(*gens):
    live = list(gens)
    while live:
        for gen in list(live):
            try:
                next(gen)
            except StopIteration:
                live.remove(gen)


def _mamba_kernel(h_ref, g_ref, win_ref, wdt_ref, wdtT_ref, cw_ref, cb_ref,
                  dtb_ref, dtbT_ref, alog_ref, alogT_ref, dskip_ref, gn_ref, wo_ref,
                  o_ref, xpad_ref, state_ref, y_ref, *handover, n_heads, tiles_per_seq):
    slots = (handover[:N_HANDOVER], handover[N_HANDOVER:])
    s = pl.program_id(0)
    pad = V7X_SUBLANES

    @pl.when(s == 0)
    def _():
        for ref in slots[1]:
            ref[...] = jnp.zeros_like(ref)

    @pl.when(lax.rem(s, tiles_per_seq) == 0)
    def _():
        xpad_ref[:, 0:pad, :] = jnp.zeros((xpad_ref.shape[0], pad, xpad_ref.shape[2]), F32)

    @pl.when((s == 0) | (lax.rem(s - 1, tiles_per_seq) == 0))
    def _():
        state_ref[...] = jnp.zeros_like(state_ref)

    def step(write_slot):
        _interleave(
            _mamba_stage_a(h_ref, g_ref, win_ref, wdt_ref, wdtT_ref, cw_ref, cb_ref,
                           dtb_ref, dtbT_ref, alog_ref, alogT_ref, xpad_ref, slots[write_slot],
                           n_heads=n_heads),
            _mamba_stage_b(slots[1 - write_slot], dskip_ref, gn_ref, wo_ref, o_ref, state_ref, y_ref,
                           n_heads=n_heads))

    parity = lax.rem(s, 2)

    @pl.when(parity == 0)
    def _():
        step(0)

    @pl.when(parity == 1)
    def _():
        step(1)


def _mamba_block(h3, norm_g, w_in, conv_w, conv_b, dt_bias, a_log, d_skip, gate_norm, w_out):
    bsz, t, d = h3.shape
    n_heads = dt_bias.shape[0]
    d_inner = n_heads * SSM_HEAD_DIM
    conv_dim = d_inner + 2 * SSM_GROUPS * SSM_STATE
    pw = 2 * SSM_HEAD_DIM
    assert w_in.shape == (d, d_inner + conv_dim + n_heads)
    assert n_heads % (2 * SSM_GROUPS) == 0 and n_heads <= V7X_LANES
    assert pw == V7X_LANES and SSM_STATE == V7X_LANES and (conv_dim // pw) % 2 == 0
    L = SSM_TOKEN_TILE
    assert t % L == 0 and L % SSM_SUBCHUNK == 0
    hpg = n_heads // SSM_GROUPS
    tiles_per_seq = t // L
    n_tiles = bsz * tiles_per_seq

    w_in16 = w_in.astype(BF16)
    wdt = w_in16[:, d_inner + conv_dim:]
    wdtT = wdt.T
    lane_pad = V7X_LANES - n_heads
    wdt_pad = jnp.pad(wdt, ((0, 0), (0, lane_pad)))
    dskip_row = jnp.repeat(d_skip.astype(F32), SSM_HEAD_DIM).reshape(1, d_inner)

    arrays = [
        norm_g.reshape(1, d),
        w_in16, wdt_pad, wdtT,
        conv_w.astype(F32), conv_b.reshape(1, conv_dim).astype(F32),
        jnp.pad(dt_bias.reshape(1, n_heads).astype(F32), ((0, 0), (0, lane_pad))),
        dt_bias.reshape(n_heads, 1).astype(F32),
        jnp.pad(a_log.reshape(1, n_heads).astype(F32), ((0, 0), (0, lane_pad))),
        a_log.reshape(n_heads, 1).astype(F32),
        dskip_row,
        gate_norm.reshape(1, d_inner).astype(F32),
        w_out.astype(BF16),
    ]
    handover = [
        pltpu.VMEM((L, d), F32),
        pltpu.VMEM((L, d_inner), F32),
        pltpu.VMEM((d_inner // pw, L, pw), F32),
        pltpu.VMEM((SSM_GROUPS, L, SSM_STATE), BF16),
        pltpu.VMEM((SSM_GROUPS, L, SSM_STATE), BF16),
        pltpu.VMEM((L, V7X_LANES), F32),
        pltpu.VMEM((L, V7X_LANES), F32),
        pltpu.VMEM((n_heads, L), F32),
    ]
    assert len(handover) == N_HANDOVER
    in_specs = [pl.BlockSpec((L, d), lambda s: (jnp.minimum(s, n_tiles - 1), 0))]
    in_specs += [_const_spec(a.shape) for a in arrays]
    out = pl.pallas_call(
        functools.partial(_mamba_kernel, n_heads=n_heads, tiles_per_seq=tiles_per_seq),
        out_shape=jax.ShapeDtypeStruct((bsz * t, d), F32),
        grid=(n_tiles + 1,),
        in_specs=in_specs,
        out_specs=pl.BlockSpec((L, d), lambda s: (jnp.maximum(s - 1, 0), 0)),
        scratch_shapes=[
            pltpu.VMEM((conv_dim // pw, L + V7X_SUBLANES, pw), F32),
            pltpu.VMEM((SSM_GROUPS, SSM_STATE, hpg * SSM_HEAD_DIM), F32),
            pltpu.VMEM((L, d_inner), F32),
        ] + handover + handover,
        compiler_params=pltpu.CompilerParams(
            dimension_semantics=("arbitrary",),
            vmem_limit_bytes=V7X_VMEM_LIMIT_BYTES),
        name="mamba2_block",
    )(h3.reshape(bsz * t, d), *arrays)
    return out.reshape(bsz, t, d)


def _t5_bucket_table(blk):
    qi = jnp.arange(blk)[:, None] + blk
    kj = jnp.arange(2 * blk)[None, :]
    dist = qi - kj
    n = jnp.maximum(dist, 0)
    max_exact = REL_BUCKETS // 2
    nf = jnp.maximum(n, 1).astype(F32)
    large = max_exact + (jnp.log(nf / max_exact) / math.log(ATT_WINDOW / max_exact)
                         * (REL_BUCKETS - max_exact)).astype(jnp.int32)
    large = jnp.minimum(large, REL_BUCKETS - 1)
    bucket = jnp.where(n < max_exact, n, large)
    in_window = (dist >= 0) & (dist < ATT_WINDOW)
    return jnp.where(in_window, bucket, -1).astype(jnp.int32)


def _attn_kernel(relb_ref, sink_ref, h_ref, g_ref, wq_ref, qn_ref, kp_ref, kc_ref, vp_ref, vc_ref,
                 bucket_ref, wo_ref, o_ref, bias_ref, *, n_heads):
    tq = h_ref.shape[0]
    blk = ATT_WINDOW
    hd = ATT_HEAD_DIM
    pw = 2 * hd
    group = n_heads // ATT_KV_HEADS
    pairs = group // 2
    first = (pl.program_id(0) == 0) & (pl.program_id(1) == 0)

    @pl.when(first)
    def _():
        bias_ref[...] = jnp.full(bias_ref.shape, -jnp.inf, F32)
        bucket = bucket_ref[...]

        def fill(b, carry):
            hit = bucket == b
            for hh in range(n_heads):
                bias_ref[hh] = jnp.where(hit, relb_ref[b, hh] * LOG2E, bias_ref[hh])
            return carry

        lax.fori_loop(0, REL_BUCKETS, fill, 0)

    h = h_ref[...]
    u = _rms_scale(h, g_ref[...]).astype(BF16)
    q = jnp.dot(u, wq_ref[...].astype(BF16), preferred_element_type=F32)
    lane_lo = _pair_lane_mask()

    q16 = jnp.concatenate(
        [(q[:, p * pw:(p + 1) * pw]
          * _pair_rms_scale(q[:, p * pw:(p + 1) * pw], lane_lo, hd ** -0.5 * LOG2E)).astype(BF16)
         for p in range(n_heads // 2)], axis=1)

    kall = jnp.concatenate([kp_ref[...], kc_ref[...]], axis=0).astype(F32) * qn_ref[...]
    vall = jnp.concatenate([vp_ref[...], vc_ref[...]], axis=0).astype(F32)
    kswap = pltpu.roll(kall, hd, 1)
    vswap = pltpu.roll(vall, hd, 1)
    k_slots = []
    v_slots = []
    for kvh in range(ATT_KV_HEADS):
        src_k, alt_k = (kall, kswap) if kvh == 0 else (kswap, kall)
        src_v, alt_v = (vall, vswap) if kvh == 0 else (vswap, vall)
        k_slots.append((jnp.where(lane_lo, src_k, 0.0).astype(BF16), jnp.where(lane_lo, 0.0, alt_k).astype(BF16)))
        v_slots.append((jnp.where(lane_lo, src_v, 0.0).astype(BF16), jnp.where(lane_lo, 0.0, alt_v).astype(BF16)))

    kj = lax.broadcasted_iota(jnp.int32, (1, 2 * blk), 1)
    key_ok = kj >= jnp.where(pl.program_id(1) > 0, 0, blk)

    o_rows = []
    for r in range(tq // blk):
        qrows = slice(r * blk, (r + 1) * blk)
        krows = slice(r * blk, r * blk + 2 * blk)
        out_pairs = []
        for kvh in range(ATT_KV_HEADS):
            q_stack = jnp.concatenate(
                [q16[qrows, (kvh * pairs + j) * pw:(kvh * pairs + j + 1) * pw] for j in range(pairs)],
                axis=0)
            s_slots = [lax.dot_general(q_stack, ks[krows, :], (((1,), (1,)), ((), ())),
                                       preferred_element_type=F32) for ks in k_slots[kvh]]
            v_stack = jnp.concatenate([vs[krows, :] for vs in v_slots[kvh]], axis=0)
            p_rows = []
            scales = []
            for j in range(pairs):
                probs = []
                rinv = []
                for e in range(2):
                    hh = kvh * group + 2 * j + e
                    s = s_slots[e][j * blk:(j + 1) * blk, :] + bias_ref[hh]
                    if r == 0:
                        s = jnp.where(key_ok, s, -jnp.inf)
                    sink = sink_ref[hh] * LOG2E
                    m = jnp.maximum(jnp.max(s, axis=-1, keepdims=True), sink)
                    pexp = jnp.exp2(s - m)
                    denom = jnp.sum(pexp, axis=-1, keepdims=True) + jnp.exp2(sink - m)
                    probs.append(pexp.astype(BF16))
                    rinv.append(1.0 / denom)
                p_rows.append(jnp.concatenate(probs, axis=1))
                scales.append(jnp.where(lane_lo, rinv[0], rinv[1]))
            pv = jnp.dot(jnp.concatenate(p_rows, axis=0), v_stack, preferred_element_type=F32)
            for j in range(pairs):
                out_pairs.append((pv[j * blk:(j + 1) * blk, :] * scales[j]).astype(BF16))
        o_rows.append(jnp.concatenate(out_pairs, axis=1))
    o = jnp.concatenate(o_rows, axis=0)
    o_ref[...] = h + jnp.dot(o, wo_ref[...].astype(BF16), preferred_element_type=F32)


def _attention_block(h3, k, v, norm_g, w_q, q_norm, sinks, rel_bias, w_o):
    bsz, t, d = h3.shape
    n_heads = sinks.shape[0]
    hd = ATT_HEAD_DIM
    kw = ATT_KV_HEADS * hd
    blk = ATT_WINDOW
    assert w_q.shape == (d, n_heads * hd) and w_o.shape == (n_heads * hd, d)
    assert n_heads % (2 * ATT_KV_HEADS) == 0 and kw == V7X_LANES
    tq = min(ATT_TOKEN_TILE, t)
    assert t % tq == 0 and tq % blk == 0
    ratio = tq // blk
    bucket = _t5_bucket_table(blk)
    qn = jnp.tile(q_norm.astype(F32), ATT_KV_HEADS).reshape(1, kw)
    k3 = k.reshape(bsz, t, kw)
    v3 = v.reshape(bsz, t, kw)

    def prev_blk(b, i, *_):
        return (b, jnp.maximum(i * ratio - 1, 0), 0)

    def cur_blk(b, i, *_):
        return (b, i, 0)

    def const2(b, i, *_):
        return (0, 0)

    one = pl.Buffered(1)
    grid_spec = pltpu.PrefetchScalarGridSpec(
        num_scalar_prefetch=2,
        grid=(bsz, t // tq),
        in_specs=[
            pl.BlockSpec((None, tq, d), cur_blk),
            pl.BlockSpec((1, d), const2, pipeline_mode=one),
            pl.BlockSpec((d, n_heads * hd), const2, pipeline_mode=one),
            pl.BlockSpec((1, kw), const2, pipeline_mode=one),
            pl.BlockSpec((None, blk, kw), prev_blk),
            pl.BlockSpec((None, tq, kw), cur_blk),
            pl.BlockSpec((None, blk, kw), prev_blk),
            pl.BlockSpec((None, tq, kw), cur_blk),
            pl.BlockSpec((blk, 2 * blk), const2, pipeline_mode=one),
            pl.BlockSpec((n_heads * hd, d), const2, pipeline_mode=one),
        ],
        out_specs=pl.BlockSpec((None, tq, d), cur_blk),
        scratch_shapes=[pltpu.VMEM((n_heads, blk, 2 * blk), F32)],
    )
    return pl.pallas_call(
        functools.partial(_attn_kernel, n_heads=n_heads),
        out_shape=jax.ShapeDtypeStruct((bsz, t, d), F32),
        grid_spec=grid_spec,
        compiler_params=pltpu.CompilerParams(
            dimension_semantics=("arbitrary", "arbitrary"),
            vmem_limit_bytes=V7X_VMEM_LIMIT_BYTES),
        name="swa_block",
    )(rel_bias.astype(F32), sinks.astype(F32), h3, norm_g.reshape(1, d), w_q.astype(F32), qn,
      k3, k3, v3, v3, bucket, w_o.astype(F32))


def kernel(x, ffn_norm, ffn_w1, ffn_w3, ffn_w2, ssm_norm, ssm_w_in, ssm_conv_w, ssm_conv_b, ssm_dt_bias, ssm_a_log, ssm_d, ssm_gate_norm, ssm_w_out, kv_norm, w_kv, k_norm, attn_norm, w_q, q_norm, sinks, w_o, rel_bias):
    bsz, t, d = x.shape
    depth = ffn_norm.shape[0]
    n_a = ssm_norm.shape[0]
    n = bsz * t
    w1 = ffn_w1.astype(F32)
    w3 = ffn_w3.astype(F32)
    w2 = ffn_w2.astype(F32)

    h = x
    k_shared = v_shared = None
    for layer in range(depth):
        h = _ffn_half_step(h.reshape(n, d), ffn_norm[layer, 0], w1, w3, w2, layer, 0).reshape(bsz, t, d)
        if layer < n_a:
            i = layer
            h = _mamba_block(h, ssm_norm[i], ssm_w_in[i], ssm_conv_w[i], ssm_conv_b[i], ssm_dt_bias[i],
                             ssm_a_log[i], ssm_d[i], ssm_gate_norm[i], ssm_w_out[i])
        else:
            j = layer - n_a
            h = _attention_block(h, k_shared, v_shared, attn_norm[j], w_q[j], q_norm[j], sinks[j],
                                 rel_bias, w_o[j])
        if layer == n_a - 1:
            h2, k2, v2 = _ffn_half_step(h.reshape(n, d), ffn_norm[layer, 1], w1, w3, w2, layer, 1,
                                        kv_params=(kv_norm, w_kv.astype(BF16), k_norm))
            k_shared, v_shared = k2, v2
        else:
            h2 = _ffn_half_step(h.reshape(n, d), ffn_norm[layer, 1], w1, w3, w2, layer, 1)
        h = h2.reshape(bsz, t, d)
    return h
```

```python
import functools
import math

import jax
import jax.numpy as jnp
from jax import lax
from jax.experimental import pallas as pl
from jax.experimental.pallas import tpu as pltpu

F32 = jnp.float32
BF16 = jnp.bfloat16

EPS = 1e-6
FFN_HALF = 0.5
LOG2E = 1.4426950408889634

SSM_HEAD_DIM = 64
SSM_GROUPS = 4
SSM_STATE = 128
SSM_CONV = 4
ATT_HEAD_DIM = 64
ATT_KV_HEADS = 2
ATT_WINDOW = 128
REL_BUCKETS = 32

V7X_LANES = 128
V7X_SUBLANES = 8
V7X_VMEM_LIMIT_BYTES = 56 * 1024 * 1024

FFN_TOKEN_TILE = 1024
FFN_CHUNK = 256
SSM_TOKEN_TILE = 256
SSM_SUBCHUNK = 128
ATT_TOKEN_TILE = 512


def _rms_scale(x, g):
    ms = jnp.mean(x * x, axis=-1, keepdims=True)
    return x * lax.rsqrt(ms + EPS) * g


def _sigmoid(x):
    return 0.5 * jnp.tanh(0.5 * x) + 0.5


def _silu(x):
    return x * _sigmoid(x)


def _const_spec(shape):
    zeros = (0,) * len(shape)
    return pl.BlockSpec(shape, lambda *_: zeros, pipeline_mode=pl.Buffered(1))


def _pair_lane_mask():
    return lax.broadcasted_iota(jnp.int32, (1, 2 * ATT_HEAD_DIM), 1) < ATT_HEAD_DIM


def _pair_rms_scale(x, lane_lo, post):
    xsq = x * x
    s_lo = jnp.sum(jnp.where(lane_lo, xsq, 0.0), axis=-1, keepdims=True)
    s_hi = jnp.sum(jnp.where(lane_lo, 0.0, xsq), axis=-1, keepdims=True)
    inv = 1.0 / ATT_HEAD_DIM
    return jnp.where(lane_lo, lax.rsqrt(s_lo * inv + EPS) * post, lax.rsqrt(s_hi * inv + EPS) * post)


def _ffn_kernel(*refs, n_chunks, fc, with_kv):
    if with_kv:
        h_ref, g_ref, w1_ref, w3_ref, w2_ref, kvg_ref, wkv_ref, kn_ref, o_ref, k_ref, v_ref = refs
    else:
        h_ref, g_ref, w1_ref, w3_ref, w2_ref, o_ref = refs
    h = h_ref[...]
    u = _rms_scale(h, g_ref[...]).astype(BF16)
    for c in range(n_chunks):
        cols = slice(c * fc, (c + 1) * fc)
        a = jnp.dot(u, w1_ref[:, cols].astype(BF16), preferred_element_type=F32)
        b = jnp.dot(u, w3_ref[:, cols].astype(BF16), preferred_element_type=F32)
        gate = (_silu(a) * b).astype(BF16)
        d = jnp.dot(gate, w2_ref[cols, :].astype(BF16), preferred_element_type=F32)
        if c == 0:
            o_ref[...] = d
        else:
            o_ref[...] += d
    out = h + FFN_HALF * o_ref[...]
    o_ref[...] = out
    if with_kv:
        kw = k_ref.shape[1]
        u2 = _rms_scale(out, kvg_ref[...]).astype(BF16)
        kv = jnp.dot(u2, wkv_ref[...], preferred_element_type=F32)
        k = kv[:, :kw]
        k_ref[...] = (k * _pair_rms_scale(k, _pair_lane_mask(), 1.0) * kn_ref[...]).astype(k_ref.dtype)
        v_ref[...] = kv[:, kw:].astype(v_ref.dtype)


def _ffn_half_step(h, g, w1, w3, w2, layer, half, kv_params=None):
    n, d = h.shape
    f = w1.shape[-1]
    tm = min(FFN_TOKEN_TILE, n)
    fc = FFN_CHUNK
    assert n % tm == 0 and f % fc == 0
    with_kv = kv_params is not None

    def wsel(*_):
        return (layer, half, 0, 0)

    one = pl.Buffered(1)
    row = pl.BlockSpec((tm, d), lambda i: (i, 0))
    in_specs = [
        row,
        _const_spec((1, d)),
        pl.BlockSpec((None, None, d, f), wsel, pipeline_mode=one),
        pl.BlockSpec((None, None, d, f), wsel, pipeline_mode=one),
        pl.BlockSpec((None, None, f, d), wsel, pipeline_mode=one),
    ]
    args = [h, g.reshape(1, d), w1, w3, w2]
    out_shape = jax.ShapeDtypeStruct((n, d), F32)
    out_specs = row
    if with_kv:
        kv_norm, w_kv16, k_norm = kv_params
        kw = ATT_KV_HEADS * ATT_HEAD_DIM
        assert w_kv16.shape == (d, 2 * kw) and kw == V7X_LANES
        kn = jnp.tile(k_norm.astype(F32), ATT_KV_HEADS).reshape(1, kw)
        in_specs += [_const_spec((1, d)), _const_spec((d, 2 * kw)), _const_spec((1, kw))]
        args += [kv_norm.reshape(1, d), w_kv16, kn]
        kv_spec = pl.BlockSpec((tm, kw), lambda i: (i, 0))
        out_shape = (out_shape, jax.ShapeDtypeStruct((n, kw), BF16), jax.ShapeDtypeStruct((n, kw), BF16))
        out_specs = (row, kv_spec, kv_spec)
    return pl.pallas_call(
        functools.partial(_ffn_kernel, n_chunks=f // fc, fc=fc, with_kv=with_kv),
        out_shape=out_shape,
        grid=(n // tm,),
        in_specs=in_specs,
        out_specs=out_specs,
        compiler_params=pltpu.CompilerParams(
            dimension_semantics=("arbitrary",), vmem_limit_bytes=V7X_VMEM_LIMIT_BYTES),
        name="ffn_kv_half_step" if with_kv else "ffn_half_step",
    )(*args)


N_HANDOVER = 8


def _pair_cols(v, r0, lane_lo):
    if v.shape[0] == 1:
        return jnp.where(lane_lo, v[:, r0:r0 + 1], v[:, r0 + 1:r0 + 2])
    idx = jnp.where(jnp.broadcast_to(lane_lo, v.shape), r0, r0 + 1)
    return jnp.take_along_axis(v, idx, axis=1)


def _bf16_terms(x):
    hi = x.astype(BF16)
    r1 = x - hi.astype(F32)
    mid = r1.astype(BF16)
    lo = (r1 - mid.astype(F32)).astype(BF16)
    return hi, mid, lo


def _exact_f32_dot(a, b, split_rhs):
    if split_rhs:
        n = b.shape[1]
        out = jnp.dot(a, jnp.concatenate(_bf16_terms(b), axis=1), preferred_element_type=F32)
        return out[:, :n] + out[:, n:2 * n] + out[:, 2 * n:]
    m = a.shape[0]
    out = jnp.dot(jnp.concatenate(_bf16_terms(a), axis=0), b, preferred_element_type=F32)
    return out[:m] + out[m:2 * m] + out[2 * m:]


def _mamba_stage_a(h_ref, g_ref, win_ref, wdt_ref, wdtT_ref, cw_ref, cb_ref,
                   dtb_ref, dtbT_ref, alog_ref, alogT_ref, xpad_ref, bufs, *, n_heads):
    hs_ref, z_ref, xs_ref, b_ref, c_ref, dt_ref, da_ref, daT_ref = bufs
    L = h_ref.shape[0]
    d_inner = z_ref.shape[1]
    pw = 2 * SSM_HEAD_DIM
    n_slab = xpad_ref.shape[0]
    n_xs = xs_ref.shape[0]
    pad = V7X_SUBLANES
    zc = 4 * pw

    h = h_ref[...]
    hs_ref[...] = h
    u = _rms_scale(h, g_ref[...]).astype(BF16)
    yield

    def softplus(x):
        return jnp.maximum(x, 0.0) + jnp.log1p(jnp.exp(-jnp.abs(x)))

    x0 = d_inner
    dt_raw = jnp.dot(u, wdt_ref[...], preferred_element_type=F32)
    dt_rawT = lax.dot_general(wdtT_ref[...], u, (((1,), (1,)), ((), ())),
                              preferred_element_type=F32)
    dt = softplus(dt_raw + dtb_ref[...])
    dtT = softplus(dt_rawT + dtbT_ref[...])
    dt_ref[...] = dt
    da_ref[...] = dt * (-jnp.exp(alog_ref[...]))
    daT_ref[...] = dtT * (-jnp.exp(alogT_ref[...]))
    yield

    def x_unit(j0):
        xj = jnp.dot(u, win_ref[:, x0 + j0 * pw:x0 + (j0 + 2) * pw], preferred_element_type=F32)
        xpad_ref[j0, pad:pad + L, :] = xj[:, :pw]
        xpad_ref[j0 + 1, pad:pad + L, :] = xj[:, pw:]
        for j in (j0, j0 + 1):
            acc = cb_ref[:, j * pw:(j + 1) * pw]
            for k in range(SSM_CONV):
                off = pad - (SSM_CONV - 1) + k
                acc = acc + cw_ref[k:k + 1, j * pw:(j + 1) * pw] * xpad_ref[j, off:off + L, :]
            xpad_ref[j, 0:pad, :] = xpad_ref[j, L:L + pad, :]
            act = _silu(acc)
            if j < n_xs:
                xs_ref[j] = act
            elif j < n_xs + SSM_GROUPS:
                b_ref[j - n_xs] = act.astype(BF16)
            else:
                c_ref[j - n_xs - SSM_GROUPS] = act.astype(BF16)

    def z_unit(c0):
        z_ref[:, c0:c0 + zc] = jnp.dot(u, win_ref[:, c0:c0 + zc], preferred_element_type=F32)

    x_units = [functools.partial(x_unit, j0) for j0 in range(0, n_slab, 2)]
    z_units = [functools.partial(z_unit, c0) for c0 in range(0, d_inner, zc)]
    order = []
    per = len(x_units) // len(z_units)
    for i, zu in enumerate(z_units):
        order += x_units[i * per:(i + 1) * per] + [zu]
    order += x_units[len(z_units) * per:]
    for unit in order:
        unit()
        yield


def _mamba_stage_b(bufs, dskip_ref, gn_ref, wo_ref, o_ref, state_ref, y_ref, *, n_heads):
    hs_ref, z_ref, xs_ref, b_ref, c_ref, dt_ref, da_ref, daT_ref = bufs
    L = hs_ref.shape[0]
    ls = SSM_SUBCHUNK
    d_inner = z_ref.shape[1]
    hp = SSM_HEAD_DIM
    pw = 2 * hp
    pairs_per_group = n_heads // SSM_GROUPS // 2

    ri = lax.broadcasted_iota(jnp.int32, (ls, ls), 0)
    ci = lax.broadcasted_iota(jnp.int32, (ls, ls), 1)
    causal = ri >= ci
    tri = jnp.where(causal, 1.0, 0.0).astype(BF16)
    triT = jnp.where(ci >= ri, 1.0, 0.0).astype(BF16)
    lane_lo = lax.broadcasted_iota(jnp.int32, (1, pw), 1) < hp
    pending = []
    state_todo = []

    def update_state():
        while state_todo:
            g_, bg_, xw_, el_ = state_todo.pop(0)
            inc = lax.dot_general(bg_, xw_, (((0,), (0,)), ((), ())),
                                  preferred_element_type=F32)
            state_ref[g_] = state_ref[g_] * el_ + inc

    def flush(keep=0):
        while len(pending) > keep:
            lhs, rhs_, off, rows_, p_ = pending.pop(0)
            y_ref[rows_, p_ * pw:(p_ + 1) * pw] = jnp.dot(lhs, rhs_, preferred_element_type=F32) + off

    for sc in range(L // ls):
        rows = slice(sc * ls, (sc + 1) * ls)
        acum = _exact_f32_dot(tri, da_ref[rows, :], split_rhs=True)
        acumT = _exact_f32_dot(daT_ref[:, rows], triT, split_rhs=False)
        a_last = acum[ls - 1:ls, :]
        e_acum = jnp.exp(acum)
        e_last = jnp.exp(a_last)
        dt_s = dt_ref[rows, :]
        w_end = jnp.exp(a_last - acum) * dt_s
        acum2 = acum * LOG2E
        acumT2 = acumT * LOG2E
        yield
        for g in range(SSM_GROUPS):
            cg = c_ref[g, rows, :]
            bg = b_ref[g, rows, :]
            cb = lax.dot_general(cg, bg, (((1,), (1,)), ((), ())),
                                 preferred_element_type=F32)
            st = state_ref[g]
            y_off = jnp.dot(cg, st.astype(BF16), preferred_element_type=F32)
            xw_parts = []
            el_parts = []
            for j in range(pairs_per_group):
                p = g * pairs_per_group + j
                r0 = 2 * p
                xs_pair = xs_ref[p, rows, :]
                xd = xs_pair * _pair_cols(dt_s, r0, lane_lo)
                rhs = jnp.concatenate([jnp.where(lane_lo, xd, 0.0), jnp.where(lane_lo, 0.0, xd)],
                                      axis=0).astype(BF16)
                ms = []
                for r in (r0, r0 + 1):
                    seg = acum2[:, r:r + 1] - acumT2[r:r + 1, :]
                    dec = jnp.exp2(jnp.where(causal, seg, -jnp.inf))
                    ms.append((cb * dec).astype(BF16))
                flush(keep=2)
                pending.append((jnp.concatenate(ms, axis=1), rhs,
                                y_off[:, j * pw:(j + 1) * pw] * _pair_cols(e_acum, r0, lane_lo), rows, p))
                xw_parts.append((xs_pair * _pair_cols(w_end, r0, lane_lo)).astype(BF16))
                el_parts.append(_pair_cols(e_last, r0, lane_lo))
                if j % 2 == 1 and j + 1 < pairs_per_group:
                    yield
            xw = jnp.concatenate(xw_parts, axis=1)
            el = jnp.concatenate(el_parts, axis=1)
            update_state()
            state_todo.append((g, bg, xw, el))
            yield

    flush()
    update_state()
    gw = d_inner // SSM_GROUPS
    slabs_per_group = gw // pw
    for g in range(SSM_GROUPS):
        cols = slice(g * gw, (g + 1) * gw)
        xs_g = jnp.concatenate([xs_ref[g * slabs_per_group + i] for i in range(slabs_per_group)], axis=1)
        yg = (y_ref[:, cols] + dskip_ref[:, cols] * xs_g) * _silu(z_ref[:, cols])
        yn = _rms_scale(yg, gn_ref[:, cols]).astype(BF16)
        part = jnp.dot(yn, wo_ref[cols, :], preferred_element_type=F32)
        if g == 0:
            o_ref[...] = hs_ref[...] + part
        else:
            o_ref[...] += part
        yield


def _interleave(*gens):
    live = list(gens)
    while live:
        for gen in list(live):
            try:
                next(gen)
            except StopIteration:
                live.remove(gen)


def _mamba_kernel(h_ref, g_ref, win_ref, wdt_ref, wdtT_ref, cw_ref, cb_ref,
                  dtb_ref, dtbT_ref, alog_ref, alogT_ref, dskip_ref, gn_ref, wo_ref,
                  o_ref, xpad_ref, state_ref, y_ref, *handover, n_heads, tiles_per_seq):
    slots = (handover[:N_HANDOVER], handover[N_HANDOVER:])
    s = pl.program_id(0)
    pad = V7X_SUBLANES

    @pl.when(s == 0)
    def _():
        for ref in slots[1]:
            ref[...] = jnp.zeros_like(ref)

    @pl.when(lax.rem(s, tiles_per_seq) == 0)
    def _():
        xpad_ref[:, 0:pad, :] = jnp.zeros((xpad_ref.shape[0], pad, xpad_ref.shape[2]), F32)

    @pl.when((s == 0) | (lax.rem(s - 1, tiles_per_seq) == 0))
    def _():
        state_ref[...] = jnp.zeros_like(state_ref)

    def step(write_slot):
        _interleave(
            _mamba_stage_b(slots[1 - write_slot], dskip_ref, gn_ref, wo_ref, o_ref, state_ref, y_ref,
                           n_heads=n_heads),
            _mamba_stage_a(h_ref, g_ref, win_ref, wdt_ref, wdtT_ref, cw_ref, cb_ref,
                           dtb_ref, dtbT_ref, alog_ref, alogT_ref, xpad_ref, slots[write_slot],
                           n_heads=n_heads))

    parity = lax.rem(s, 2)

    @pl.when(parity == 0)
    def _():
        step(0)

    @pl.when(parity == 1)
    def _():
        step(1)


def _mamba_block(h3, norm_g, w_in, conv_w, conv_b, dt_bias, a_log, d_skip, gate_norm, w_out):
    bsz, t, d = h3.shape
    n_heads = dt_bias.shape[0]
    d_inner = n_heads * SSM_HEAD_DIM
    conv_dim = d_inner + 2 * SSM_GROUPS * SSM_STATE
    pw = 2 * SSM_HEAD_DIM
    assert w_in.shape == (d, d_inner + conv_dim + n_heads)
    assert n_heads % (2 * SSM_GROUPS) == 0 and n_heads <= V7X_LANES
    assert pw == V7X_LANES and SSM_STATE == V7X_LANES and (conv_dim // pw) % 2 == 0
    L = SSM_TOKEN_TILE
    assert t % L == 0 and L % SSM_SUBCHUNK == 0
    hpg = n_heads // SSM_GROUPS
    tiles_per_seq = t // L
    n_tiles = bsz * tiles_per_seq

    w_in16 = w_in.astype(BF16)
    wdt = w_in16[:, d_inner + conv_dim:]
    wdtT = wdt.T
    lane_pad = V7X_LANES - n_heads
    wdt_pad = jnp.pad(wdt, ((0, 0), (0, lane_pad)))
    dskip_row = jnp.repeat(d_skip.astype(F32), SSM_HEAD_DIM).reshape(1, d_inner)

    arrays = [
        norm_g.reshape(1, d),
        w_in16, wdt_pad, wdtT,
        conv_w.astype(F32), conv_b.reshape(1, conv_dim).astype(F32),
        jnp.pad(dt_bias.reshape(1, n_heads).astype(F32), ((0, 0), (0, lane_pad))),
        dt_bias.reshape(n_heads, 1).astype(F32),
        jnp.pad(a_log.reshape(1, n_heads).astype(F32), ((0, 0), (0, lane_pad))),
        a_log.reshape(n_heads, 1).astype(F32),
        dskip_row,
        gate_norm.reshape(1, d_inner).astype(F32),
        w_out.astype(BF16),
    ]
    handover = [
        pltpu.VMEM((L, d), F32),
        pltpu.VMEM((L, d_inner), F32),
        pltpu.VMEM((d_inner // pw, L, pw), F32),
        pltpu.VMEM((SSM_GROUPS, L, SSM_STATE), BF16),
        pltpu.VMEM((SSM_GROUPS, L, SSM_STATE), BF16),
        pltpu.VMEM((L, V7X_LANES), F32),
        pltpu.VMEM((L, V7X_LANES), F32),
        pltpu.VMEM((n_heads, L), F32),
    ]
    assert len(handover) == N_HANDOVER
    in_specs = [pl.BlockSpec((L, d), lambda s: (jnp.minimum(s, n_tiles - 1), 0))]
    in_specs += [_const_spec(a.shape) for a in arrays]
    out = pl.pallas_call(
        functools.partial(_mamba_kernel, n_heads=n_heads, tiles_per_seq=tiles_per_seq),
        out_shape=jax.ShapeDtypeStruct((bsz * t, d), F32),
        grid=(n_tiles + 1,),
        in_specs=in_specs,
        out_specs=pl.BlockSpec((L, d), lambda s: (jnp.maximum(s - 1, 0), 0)),
        scratch_shapes=[
            pltpu.VMEM((conv_dim // pw, L + V7X_SUBLANES, pw), F32),
            pltpu.VMEM((SSM_GROUPS, SSM_STATE, hpg * SSM_HEAD_DIM), F32),
            pltpu.VMEM((L, d_inner), F32),
        ] + handover + handover,
        compiler_params=pltpu.CompilerParams(
            dimension_semantics=("arbitrary",),
            vmem_limit_bytes=V7X_VMEM_LIMIT_BYTES),
        name="mamba2_block",
    )(h3.reshape(bsz * t, d), *arrays)
    return out.reshape(bsz, t, d)


def _t5_bucket_table(blk):
    qi = jnp.arange(blk)[:, None] + blk
    kj = jnp.arange(2 * blk)[None, :]
    dist = qi - kj
    n = jnp.maximum(dist, 0)
    max_exact = REL_BUCKETS // 2
    nf = jnp.maximum(n, 1).astype(F32)
    large = max_exact + (jnp.log(nf / max_exact) / math.log(ATT_WINDOW / max_exact)
                         * (REL_BUCKETS - max_exact)).astype(jnp.int32)
    large = jnp.minimum(large, REL_BUCKETS - 1)
    bucket = jnp.where(n < max_exact, n, large)
    in_window = (dist >= 0) & (dist < ATT_WINDOW)
    return jnp.where(in_window, bucket, -1).astype(jnp.int32)


def _attn_kernel(relb_ref, sink_ref, h_ref, g_ref, wq_ref, qn_ref, kp_ref, kc_ref, vp_ref, vc_ref,
                 bucket_ref, wo_ref, o_ref, bias_ref, *, n_heads):
    tq = h_ref.shape[0]
    blk = ATT_WINDOW
    hd = ATT_HEAD_DIM
    pw = 2 * hd
    group = n_heads // ATT_KV_HEADS
    pairs = group // 2
    first = (pl.program_id(0) == 0) & (pl.program_id(1) == 0)

    @pl.when(first)
    def _():
        bias_ref[...] = jnp.full(bias_ref.shape, -jnp.inf, F32)
        bucket = bucket_ref[...]

        def fill(b, carry):
            hit = bucket == b
            for hh in range(n_heads):
                bias_ref[hh] = jnp.where(hit, relb_ref[b, hh] * LOG2E, bias_ref[hh])
            return carry

        lax.fori_loop(0, REL_BUCKETS, fill, 0)

    h = h_ref[...]
    u = _rms_scale(h, g_ref[...]).astype(BF16)
    q = jnp.dot(u, wq_ref[...].astype(BF16), preferred_element_type=F32)
    lane_lo = _pair_lane_mask()

    q16 = jnp.concatenate(
        [(q[:, p * pw:(p + 1) * pw]
          * _pair_rms_scale(q[:, p * pw:(p + 1) * pw], lane_lo, hd ** -0.5 * LOG2E)).astype(BF16)
         for p in range(n_heads // 2)], axis=1)

    kall = jnp.concatenate([kp_ref[...], kc_ref[...]], axis=0).astype(F32) * qn_ref[...]
    vall = jnp.concatenate([vp_ref[...], vc_ref[...]], axis=0).astype(F32)
    kswap = pltpu.roll(kall, hd, 1)
    vswap = pltpu.roll(vall, hd, 1)
    k_slots = []
    v_slots = []
    for kvh in range(ATT_KV_HEADS):
        src_k, alt_k = (kall, kswap) if kvh == 0 else (kswap, kall)
        src_v, alt_v = (vall, vswap) if kvh == 0 else (vswap, vall)
        k_slots.append((jnp.where(lane_lo, src_k, 0.0).astype(BF16), jnp.where(lane_lo, 0.0, alt_k).astype(BF16)))
        v_slots.append((jnp.where(lane_lo, src_v, 0.0).astype(BF16), jnp.where(lane_lo, 0.0, alt_v).astype(BF16)))

    kj = lax.broadcasted_iota(jnp.int32, (1, 2 * blk), 1)
    key_ok = kj >= jnp.where(pl.program_id(1) > 0, 0, blk)

    o_rows = []
    for r in range(tq // blk):
        qrows = slice(r * blk, (r + 1) * blk)
        krows = slice(r * blk, r * blk + 2 * blk)
        out_pairs = []
        for kvh in range(ATT_KV_HEADS):
            q_stack = jnp.concatenate(
                [q16[qrows, (kvh * pairs + j) * pw:(kvh * pairs + j + 1) * pw] for j in range(pairs)],
                axis=0)
            s_slots = [lax.dot_general(q_stack, ks[krows, :], (((1,), (1,)), ((), ())),
                                       preferred_element_type=F32) for ks in k_slots[kvh]]
            v_stack = jnp.concatenate([vs[krows, :] for vs in v_slots[kvh]], axis=0)
            p_rows = []
            scales = []
            for j in range(pairs):
                probs = []
                rinv = []
                for e in range(2):
                    hh = kvh * group + 2 * j + e
                    s = s_slots[e][j * blk:(j + 1) * blk, :] + bias_ref[hh]
                    if r == 0:
                        s = jnp.where(key_ok, s, -jnp.inf)
                    sink = sink_ref[hh] * LOG2E
                    m = jnp.maximum(jnp.max(s, axis=-1, keepdims=True), sink)
                    pexp = jnp.exp2(s - m)
                    denom = jnp.sum(pexp, axis=-1, keepdims=True) + jnp.exp2(sink - m)
                    probs.append(pexp.astype(BF16))
                    rinv.append(1.0 / denom)
                p_rows.append(jnp.concatenate(probs, axis=1))
                scales.append(jnp.where(lane_lo, rinv[0], rinv[1]))
            pv = jnp.dot(jnp.concatenate(p_rows, axis=0), v_stack, preferred_element_type=F32)
            for j in range(pairs):
                out_pairs.append((pv[j * blk:(j + 1) * blk, :] * scales[j]).astype(BF16))
        o_rows.append(jnp.concatenate(out_pairs, axis=1))
    o = jnp.concatenate(o_rows, axis=0)
    o_ref[...] = h + jnp.dot(o, wo_ref[...].astype(BF16), preferred_element_type=F32)


def _attention_block(h3, k, v, norm_g, w_q, q_norm, sinks, rel_bias, w_o):
    bsz, t, d = h3.shape
    n_heads = sinks.shape[0]
    hd = ATT_HEAD_DIM
    kw = ATT_KV_HEADS * hd
    blk = ATT_WINDOW
    assert w_q.shape == (d, n_heads * hd) and w_o.shape == (n_heads * hd, d)
    assert n_heads % (2 * ATT_KV_HEADS) == 0 and kw == V7X_LANES
    tq = min(ATT_TOKEN_TILE, t)
    assert t % tq == 0 and tq % blk == 0
    ratio = tq // blk
    bucket = _t5_bucket_table(blk)
    qn = jnp.tile(q_norm.astype(F32), ATT_KV_HEADS).reshape(1, kw)
    k3 = k.reshape(bsz, t, kw)
    v3 = v.reshape(bsz, t, kw)

    def prev_blk(b, i, *_):
        return (b, jnp.maximum(i * ratio - 1, 0), 0)

    def cur_blk(b, i, *_):
        return (b, i, 0)

    def const2(b, i, *_):
        return (0, 0)

    one = pl.Buffered(1)
    grid_spec = pltpu.PrefetchScalarGridSpec(
        num_scalar_prefetch=2,
        grid=(bsz, t // tq),
        in_specs=[
            pl.BlockSpec((None, tq, d), cur_blk),
            pl.BlockSpec((1, d), const2, pipeline_mode=one),
            pl.BlockSpec((d, n_heads * hd), const2, pipeline_mode=one),
            pl.BlockSpec((1, kw), const2, pipeline_mode=one),
            pl.BlockSpec((None, blk, kw), prev_blk),
            pl.BlockSpec((None, tq, kw), cur_blk),
            pl.BlockSpec((None, blk, kw), prev_blk),
            pl.BlockSpec((None, tq, kw), cur_blk),
            pl.BlockSpec((blk, 2 * blk), const2, pipeline_mode=one),
            pl.BlockSpec((n_heads * hd, d), const2, pipeline_mode=one),
        ],
        out_specs=pl.BlockSpec((None, tq, d), cur_blk),
        scratch_shapes=[pltpu.VMEM((n_heads, blk, 2 * blk), F32)],
    )
    return pl.pallas_call(
        functools.partial(_attn_kernel, n_heads=n_heads),
        out_shape=jax.ShapeDtypeStruct((bsz, t, d), F32),
        grid_spec=grid_spec,
        compiler_params=pltpu.CompilerParams(
            dimension_semantics=("arbitrary", "arbitrary"),
            vmem_limit_bytes=V7X_VMEM_LIMIT_BYTES),
        name="swa_block",
    )(rel_bias.astype(F32), sinks.astype(F32), h3, norm_g.reshape(1, d), w_q.astype(F32), qn,
      k3, k3, v3, v3, bucket, w_o.astype(F32))


def kernel(x, ffn_norm, ffn_w1, ffn_w3, ffn_w2, ssm_norm, ssm_w_in, ssm_conv_w, ssm_conv_b, ssm_dt_bias, ssm_a_log, ssm_d, ssm_gate_norm, ssm_w_out, kv_norm, w_kv, k_norm, attn_norm, w_q, q_norm, sinks, w_o, rel_bias):
    bsz, t, d = x.shape
    depth = ffn_norm.shape[0]
    n_a = ssm_norm.shape[0]
    n = bsz * t
    w1 = ffn_w1.astype(F32)
    w3 = ffn_w3.astype(F32)
    w2 = ffn_w2.astype(F32)

    h = x
    k_shared = v_shared = None
    for layer in range(depth):
        h = _ffn_half_step(h.reshape(n, d), ffn_norm[layer, 0], w1, w3, w2, layer, 0).reshape(bsz, t, d)
        if layer < n_a:
            i = layer
            h = _mamba_block(h, ssm_norm[i], ssm_w_in[i], ssm_conv_w[i], ssm_conv_b[i], ssm_dt_bias[i],
                             ssm_a_log[i], ssm_d[i], ssm_gate_norm[i], ssm_w_out[i])
        else:
            j = layer - n_a
            h = _attention_block(h, k_shared, v_shared, attn_norm[j], w_q[j], q_norm[j], sinks[j],
                                 rel_bias, w_o[j])
        if layer == n_a - 1:
            h2, k2, v2 = _ffn_half_step(h.reshape(n, d), ffn_norm[layer, 1], w1, w3, w2, layer, 1,
                                        kv_params=(kv_norm, w_kv.astype(BF16), k_norm))
            k_shared, v_shared = k2, v2
        else:
            h2 = _ffn_half_step(h.reshape(n, d), ffn_norm[layer, 1], w1, w3, w2, layer, 1)
        h = h2.reshape(bsz, t, d)
    return h
```

```python
import functools
import math

import jax
import jax.numpy as jnp
from jax import lax
from jax.experimental import pallas as pl
from jax.experimental.pallas import tpu as pltpu

F32 = jnp.float32
BF16 = jnp.bfloat16

EPS = 1e-6
FFN_HALF = 0.5
LOG2E = 1.4426950408889634

SSM_HEAD_DIM = 64
SSM_GROUPS = 4
SSM_STATE = 128
SSM_CONV = 4
ATT_HEAD_DIM = 64
ATT_KV_HEADS = 2
ATT_WINDOW = 128
REL_BUCKETS = 32

V7X_LANES = 128
V7X_SUBLANES = 8
V7X_VMEM_LIMIT_BYTES = 56 * 1024 * 1024

FFN_TOKEN_TILE = 1024
FFN_CHUNK = 256
SSM_TOKEN_TILE = 256
SSM_SUBCHUNK = 128
ATT_TOKEN_TILE = 512


def _rms_scale(x, g):
    ms = jnp.mean(x * x, axis=-1, keepdims=True)
    return x * lax.rsqrt(ms + EPS) * g


def _silu(x):
    hx = 0.5 * x
    return hx + hx * jnp.tanh(hx)


def _const_spec(shape):
    zeros = (0,) * len(shape)
    return pl.BlockSpec(shape, lambda *_: zeros, pipeline_mode=pl.Buffered(1))


def _pair_lane_mask():
    return lax.broadcasted_iota(jnp.int32, (1, 2 * ATT_HEAD_DIM), 1) < ATT_HEAD_DIM


def _pair_rms_scale(x, lane_lo, post):
    xsq = x * x
    s_lo = jnp.sum(jnp.where(lane_lo, xsq, 0.0), axis=-1, keepdims=True)
    s_hi = jnp.sum(jnp.where(lane_lo, 0.0, xsq), axis=-1, keepdims=True)
    inv = 1.0 / ATT_HEAD_DIM
    return jnp.where(lane_lo, lax.rsqrt(s_lo * inv + EPS) * post, lax.rsqrt(s_hi * inv + EPS) * post)


def _ffn_kernel(*refs, n_chunks, fc, with_kv):
    if with_kv:
        h_ref, g_ref, w1_ref, w3_ref, w2_ref, kvg_ref, wkv_ref, kn_ref, o_ref, k_ref, v_ref = refs
    else:
        h_ref, g_ref, w1_ref, w3_ref, w2_ref, o_ref = refs
    h = h_ref[...]
    u = _rms_scale(h, g_ref[...]).astype(BF16)
    for c in range(n_chunks):
        cols = slice(c * fc, (c + 1) * fc)
        a = jnp.dot(u, w1_ref[:, cols].astype(BF16), preferred_element_type=F32)
        b = jnp.dot(u, w3_ref[:, cols].astype(BF16), preferred_element_type=F32)
        gate = (_silu(a) * b).astype(BF16)
        d = jnp.dot(gate, w2_ref[cols, :].astype(BF16), preferred_element_type=F32)
        if c == 0:
            o_ref[...] = d
        else:
            o_ref[...] += d
    out = h + FFN_HALF * o_ref[...]
    o_ref[...] = out
    if with_kv:
        kw = k_ref.shape[1]
        u2 = _rms_scale(out, kvg_ref[...]).astype(BF16)
        kv = jnp.dot(u2, wkv_ref[...], preferred_element_type=F32)
        k = kv[:, :kw]
        k_ref[...] = (k * _pair_rms_scale(k, _pair_lane_mask(), 1.0) * kn_ref[...]).astype(k_ref.dtype)
        v_ref[...] = kv[:, kw:].astype(v_ref.dtype)


def _ffn_half_step(h, g, w1, w3, w2, layer, half, kv_params=None):
    n, d = h.shape
    f = w1.shape[-1]
    tm = min(FFN_TOKEN_TILE, n)
    fc = FFN_CHUNK
    assert n % tm == 0 and f % fc == 0
    with_kv = kv_params is not None

    def wsel(*_):
        return (layer, half, 0, 0)

    one = pl.Buffered(1)
    row = pl.BlockSpec((tm, d), lambda i: (i, 0))
    in_specs = [
        row,
        _const_spec((1, d)),
        pl.BlockSpec((None, None, d, f), wsel, pipeline_mode=one),
        pl.BlockSpec((None, None, d, f), wsel, pipeline_mode=one),
        pl.BlockSpec((None, None, f, d), wsel, pipeline_mode=one),
    ]
    args = [h, g.reshape(1, d), w1, w3, w2]
    out_shape = jax.ShapeDtypeStruct((n, d), F32)
    out_specs = row
    if with_kv:
        kv_norm, w_kv16, k_norm = kv_params
        kw = ATT_KV_HEADS * ATT_HEAD_DIM
        assert w_kv16.shape == (d, 2 * kw) and kw == V7X_LANES
        kn = jnp.tile(k_norm.astype(F32), ATT_KV_HEADS).reshape(1, kw)
        in_specs += [_const_spec((1, d)), _const_spec((d, 2 * kw)), _const_spec((1, kw))]
        args += [kv_norm.reshape(1, d), w_kv16, kn]
        kv_spec = pl.BlockSpec((tm, kw), lambda i: (i, 0))
        out_shape = (out_shape, jax.ShapeDtypeStruct((n, kw), BF16), jax.ShapeDtypeStruct((n, kw), BF16))
        out_specs = (row, kv_spec, kv_spec)
    return pl.pallas_call(
        functools.partial(_ffn_kernel, n_chunks=f // fc, fc=fc, with_kv=with_kv),
        out_shape=out_shape,
        grid=(n // tm,),
        in_specs=in_specs,
        out_specs=out_specs,
        compiler_params=pltpu.CompilerParams(
            dimension_semantics=("arbitrary",), vmem_limit_bytes=V7X_VMEM_LIMIT_BYTES),
        name="ffn_kv_half_step" if with_kv else "ffn_half_step",
    )(*args)


N_HANDOVER = 8


def _pair_cols(v, r0, lane_lo):
    if v.shape[0] == 1:
        return jnp.where(lane_lo, v[:, r0:r0 + 1], v[:, r0 + 1:r0 + 2])
    idx = jnp.where(jnp.broadcast_to(lane_lo, v.shape), r0, r0 + 1)
    return jnp.take_along_axis(v, idx, axis=1)


def _bf16_terms(x):
    hi = x.astype(BF16)
    r1 = x - hi.astype(F32)
    mid = r1.astype(BF16)
    lo = (r1 - mid.astype(F32)).astype(BF16)
    return hi, mid, lo


def _exact_f32_dot(a, b, split_rhs):
    if split_rhs:
        n = b.shape[1]
        out = jnp.dot(a, jnp.concatenate(_bf16_terms(b), axis=1), preferred_element_type=F32)
        return out[:, :n] + out[:, n:2 * n] + out[:, 2 * n:]
    m = a.shape[0]
    out = jnp.dot(jnp.concatenate(_bf16_terms(a), axis=0), b, preferred_element_type=F32)
    return out[:m] + out[m:2 * m] + out[2 * m:]


def _mamba_stage_a(h_ref, g_ref, win_ref, wdt_ref, wdtT_ref, cw_ref, cb_ref,
                   dtb_ref, dtbT_ref, alog_ref, alogT_ref, xpad_ref, bufs, *, n_heads):
    hs_ref, z_ref, xs_ref, b_ref, c_ref, dt_ref, da_ref, daT_ref = bufs
    L = h_ref.shape[0]
    d_inner = z_ref.shape[1]
    pw = 2 * SSM_HEAD_DIM
    n_slab = xpad_ref.shape[0]
    n_xs = xs_ref.shape[0]
    pad = V7X_SUBLANES
    zc = 4 * pw

    h = h_ref[...]
    hs_ref[...] = h
    u = _rms_scale(h, g_ref[...]).astype(BF16)
    yield

    def softplus(x):
        return jnp.maximum(x, 0.0) + jnp.log1p(jnp.exp(-jnp.abs(x)))

    x0 = d_inner
    dt_raw = jnp.dot(u, wdt_ref[...], preferred_element_type=F32)
    dt_rawT = lax.dot_general(wdtT_ref[...], u, (((1,), (1,)), ((), ())),
                              preferred_element_type=F32)
    dt = softplus(dt_raw + dtb_ref[...])
    dtT = softplus(dt_rawT + dtbT_ref[...])
    dt_ref[...] = dt
    da_ref[...] = dt * (-jnp.exp(alog_ref[...]))
    daT_ref[...] = dtT * (-jnp.exp(alogT_ref[...]))
    yield

    def x_unit(j0):
        xj = jnp.dot(u, win_ref[:, x0 + j0 * pw:x0 + (j0 + 2) * pw], preferred_element_type=F32)
        xpad_ref[j0, pad:pad + L, :] = xj[:, :pw]
        xpad_ref[j0 + 1, pad:pad + L, :] = xj[:, pw:]
        for j in (j0, j0 + 1):
            acc = cb_ref[:, j * pw:(j + 1) * pw]
            for k in range(SSM_CONV):
                off = pad - (SSM_CONV - 1) + k
                acc = acc + cw_ref[k:k + 1, j * pw:(j + 1) * pw] * xpad_ref[j, off:off + L, :]
            xpad_ref[j, 0:pad, :] = xpad_ref[j, L:L + pad, :]
            act = _silu(acc)
            if j < n_xs:
                xs_ref[j] = act
            elif j < n_xs + SSM_GROUPS:
                b_ref[j - n_xs] = act.astype(BF16)
            else:
                c_ref[j - n_xs - SSM_GROUPS] = act.astype(BF16)

    def z_unit(c0):
        z_ref[:, c0:c0 + zc] = jnp.dot(u, win_ref[:, c0:c0 + zc], preferred_element_type=F32)

    x_units = [functools.partial(x_unit, j0) for j0 in range(0, n_slab, 2)]
    z_units = [functools.partial(z_unit, c0) for c0 in range(0, d_inner, zc)]
    order = []
    per = len(x_units) // len(z_units)
    for i, zu in enumerate(z_units):
        order += x_units[i * per:(i + 1) * per] + [zu]
    order += x_units[len(z_units) * per:]
    for unit in order:
        unit()
        yield


def _mamba_stage_b(bufs, dskip_ref, gn_ref, wo_ref, o_ref, state_ref, y_ref, *, n_heads):
    hs_ref, z_ref, xs_ref, b_ref, c_ref, dt_ref, da_ref, daT_ref = bufs
    L = hs_ref.shape[0]
    ls = SSM_SUBCHUNK
    d_inner = z_ref.shape[1]
    hp = SSM_HEAD_DIM
    pw = 2 * hp
    pairs_per_group = n_heads // SSM_GROUPS // 2

    ri = lax.broadcasted_iota(jnp.int32, (ls, ls), 0)
    ci = lax.broadcasted_iota(jnp.int32, (ls, ls), 1)
    causal = ri >= ci
    tri = jnp.where(causal, 1.0, 0.0).astype(BF16)
    triT = jnp.where(ci >= ri, 1.0, 0.0).astype(BF16)
    lane_lo = lax.broadcasted_iota(jnp.int32, (1, pw), 1) < hp
    pending = []
    state_todo = []

    def update_state():
        while state_todo:
            g_, bg_, xw_, el_ = state_todo.pop(0)
            inc = lax.dot_general(bg_, xw_, (((0,), (0,)), ((), ())),
                                  preferred_element_type=F32)
            state_ref[g_] = state_ref[g_] * el_ + inc

    def flush(keep=0):
        while len(pending) > keep:
            lhs, rhs_, off, rows_, p_ = pending.pop(0)
            y_ref[rows_, p_ * pw:(p_ + 1) * pw] = jnp.dot(lhs, rhs_, preferred_element_type=F32) + off

    for sc in range(L // ls):
        rows = slice(sc * ls, (sc + 1) * ls)
        acum = _exact_f32_dot(tri, da_ref[rows, :], split_rhs=True)
        acumT = _exact_f32_dot(daT_ref[:, rows], triT, split_rhs=False)
        a_last = acum[ls - 1:ls, :]
        e_acum = jnp.exp(acum)
        e_last = jnp.exp(a_last)
        dt_s = dt_ref[rows, :]
        w_end = jnp.exp(a_last - acum) * dt_s
        acum2 = acum * LOG2E
        acumT2 = acumT * LOG2E
        update_state()
        cbs = []
        y_offs = []
        for g in range(SSM_GROUPS):
            cg = c_ref[g, rows, :]
            cbs.append(lax.dot_general(cg, b_ref[g, rows, :], (((1,), (1,)), ((), ())),
                                       preferred_element_type=F32).astype(BF16))
            y_offs.append(jnp.dot(cg, state_ref[g].astype(BF16), preferred_element_type=F32))
        yield
        for g in range(SSM_GROUPS):
            bg = b_ref[g, rows, :]
            cb16 = cbs[g]
            y_off = y_offs[g]
            xw_parts = []
            el_parts = []
            for j in range(pairs_per_group):
                p = g * pairs_per_group + j
                r0 = 2 * p
                xs_pair = xs_ref[p, rows, :]
                xd = xs_pair * _pair_cols(dt_s, r0, lane_lo)
                rhs = jnp.concatenate([jnp.where(lane_lo, xd, 0.0), jnp.where(lane_lo, 0.0, xd)],
                                      axis=0).astype(BF16)
                ms = []
                for r in (r0, r0 + 1):
                    seg = acum2[:, r:r + 1] - acumT2[r:r + 1, :]
                    dec = jnp.exp2(jnp.where(causal, seg, -jnp.inf))
                    ms.append(cb16 * dec.astype(BF16))
                flush(keep=2)
                pending.append((jnp.concatenate(ms, axis=1), rhs,
                                y_off[:, j * pw:(j + 1) * pw] * _pair_cols(e_acum, r0, lane_lo), rows, p))
                xw_parts.append((xs_pair * _pair_cols(w_end, r0, lane_lo)).astype(BF16))
                el_parts.append(_pair_cols(e_last, r0, lane_lo))
                if j % 2 == 1 and j + 1 < pairs_per_group:
                    yield
            xw = jnp.concatenate(xw_parts, axis=1)
            el = jnp.concatenate(el_parts, axis=1)
            update_state()
            state_todo.append((g, bg, xw, el))
            yield

    flush()
    update_state()
    gw = d_inner // SSM_GROUPS
    slabs_per_group = gw // pw
    for g in range(SSM_GROUPS):
        cols = slice(g * gw, (g + 1) * gw)
        xs_g = jnp.concatenate([xs_ref[g * slabs_per_group + i] for i in range(slabs_per_group)], axis=1)
        yg = (y_ref[:, cols] + dskip_ref[:, cols] * xs_g) * _silu(z_ref[:, cols])
        yn = _rms_scale(yg, gn_ref[:, cols]).astype(BF16)
        part = jnp.dot(yn, wo_ref[cols, :], preferred_element_type=F32)
        if g == 0:
            o_ref[...] = hs_ref[...] + part
        else:
            o_ref[...] += part
        yield


def _interleave(*gens):
    live = list(gens)
    while live:
        for gen in list(live):
            try:
                next(gen)
            except StopIteration:
                live.remove(gen)


def _mamba_kernel(h_ref, g_ref, win_ref, wdt_ref, wdtT_ref, cw_ref, cb_ref,
                  dtb_ref, dtbT_ref, alog_ref, alogT_ref, dskip_ref, gn_ref, wo_ref,
                  o_ref, xpad_ref, state_ref, y_ref, *handover, n_heads, tiles_per_seq):
    slots = (handover[:N_HANDOVER], handover[N_HANDOVER:])
    s = pl.program_id(0)
    pad = V7X_SUBLANES

    @pl.when(s == 0)
    def _():
        for ref in slots[1]:
            ref[...] = jnp.zeros_like(ref)

    @pl.when(lax.rem(s, tiles_per_seq) == 0)
    def _():
        xpad_ref[:, 0:pad, :] = jnp.zeros((xpad_ref.shape[0], pad, xpad_ref.shape[2]), F32)

    @pl.when((s == 0) | (lax.rem(s - 1, tiles_per_seq) == 0))
    def _():
        state_ref[...] = jnp.zeros_like(state_ref)

    def step(write_slot):
        _interleave(
            _mamba_stage_b(slots[1 - write_slot], dskip_ref, gn_ref, wo_ref, o_ref, state_ref, y_ref,
                           n_heads=n_heads),
            _mamba_stage_a(h_ref, g_ref, win_ref, wdt_ref, wdtT_ref, cw_ref, cb_ref,
                           dtb_ref, dtbT_ref, alog_ref, alogT_ref, xpad_ref, slots[write_slot],
                           n_heads=n_heads))

    parity = lax.rem(s, 2)

    @pl.when(parity == 0)
    def _():
        step(0)

    @pl.when(parity == 1)
    def _():
        step(1)


def _mamba_block(h3, norm_g, w_in, conv_w, conv_b, dt_bias, a_log, d_skip, gate_norm, w_out):
    bsz, t, d = h3.shape
    n_heads = dt_bias.shape[0]
    d_inner = n_heads * SSM_HEAD_DIM
    conv_dim = d_inner + 2 * SSM_GROUPS * SSM_STATE
    pw = 2 * SSM_HEAD_DIM
    assert w_in.shape == (d, d_inner + conv_dim + n_heads)
    assert n_heads % (2 * SSM_GROUPS) == 0 and n_heads <= V7X_LANES
    assert pw == V7X_LANES and SSM_STATE == V7X_LANES and (conv_dim // pw) % 2 == 0
    L = SSM_TOKEN_TILE
    assert t % L == 0 and L % SSM_SUBCHUNK == 0
    hpg = n_heads // SSM_GROUPS
    tiles_per_seq = t // L
    n_tiles = bsz * tiles_per_seq

    w_in16 = w_in.astype(BF16)
    wdt = w_in16[:, d_inner + conv_dim:]
    wdtT = wdt.T
    lane_pad = V7X_LANES - n_heads
    wdt_pad = jnp.pad(wdt, ((0, 0), (0, lane_pad)))
    dskip_row = jnp.repeat(d_skip.astype(F32), SSM_HEAD_DIM).reshape(1, d_inner)

    arrays = [
        norm_g.reshape(1, d),
        w_in16, wdt_pad, wdtT,
        conv_w.astype(F32), conv_b.reshape(1, conv_dim).astype(F32),
        jnp.pad(dt_bias.reshape(1, n_heads).astype(F32), ((0, 0), (0, lane_pad))),
        dt_bias.reshape(n_heads, 1).astype(F32),
        jnp.pad(a_log.reshape(1, n_heads).astype(F32), ((0, 0), (0, lane_pad))),
        a_log.reshape(n_heads, 1).astype(F32),
        dskip_row,
        gate_norm.reshape(1, d_inner).astype(F32),
        w_out.astype(BF16),
    ]
    handover = [
        pltpu.VMEM((L, d), F32),
        pltpu.VMEM((L, d_inner), F32),
        pltpu.VMEM((d_inner // pw, L, pw), F32),
        pltpu.VMEM((SSM_GROUPS, L, SSM_STATE), BF16),
        pltpu.VMEM((SSM_GROUPS, L, SSM_STATE), BF16),
        pltpu.VMEM((L, V7X_LANES), F32),
        pltpu.VMEM((L, V7X_LANES), F32),
        pltpu.VMEM((n_heads, L), F32),
    ]
    assert len(handover) == N_HANDOVER
    in_specs = [pl.BlockSpec((L, d), lambda s: (jnp.minimum(s, n_tiles - 1), 0))]
    in_specs += [_const_spec(a.shape) for a in arrays]
    out = pl.pallas_call(
        functools.partial(_mamba_kernel, n_heads=n_heads, tiles_per_seq=tiles_per_seq),
        out_shape=jax.ShapeDtypeStruct((bsz * t, d), F32),
        grid=(n_tiles + 1,),
        in_specs=in_specs,
        out_specs=pl.BlockSpec((L, d), lambda s: (jnp.maximum(s - 1, 0), 0)),
        scratch_shapes=[
            pltpu.VMEM((conv_dim // pw, L + V7X_SUBLANES, pw), F32),
            pltpu.VMEM((SSM_GROUPS, SSM_STATE, hpg * SSM_HEAD_DIM), F32),
            pltpu.VMEM((L, d_inner), F32),
        ] + handover + handover,
        compiler_params=pltpu.CompilerParams(
            dimension_semantics=("arbitrary",),
            vmem_limit_bytes=V7X_VMEM_LIMIT_BYTES),
        name="mamba2_block",
    )(h3.reshape(bsz * t, d), *arrays)
    return out.reshape(bsz, t, d)


def _t5_bucket_table(blk):
    qi = jnp.arange(blk)[:, None] + blk
    kj = jnp.arange(2 * blk)[None, :]
    dist = qi - kj
    n = jnp.maximum(dist, 0)
    max_exact = REL_BUCKETS // 2
    nf = jnp.maximum(n, 1).astype(F32)
    large = max_exact + (jnp.log(nf / max_exact) / math.log(ATT_WINDOW / max_exact)
                         * (REL_BUCKETS - max_exact)).astype(jnp.int32)
    large = jnp.minimum(large, REL_BUCKETS - 1)
    bucket = jnp.where(n < max_exact, n, large)
    in_window = (dist >= 0) & (dist < ATT_WINDOW)
    return jnp.where(in_window, bucket, -1).astype(jnp.int32)


def _attn_kernel(relb_ref, sink_ref, h_ref, g_ref, wq_ref, qn_ref, kp_ref, kc_ref, vp_ref, vc_ref,
                 bucket_ref, wo_ref, o_ref, bias_ref, *, n_heads):
    tq = h_ref.shape[0]
    blk = ATT_WINDOW
    hd = ATT_HEAD_DIM
    pw = 2 * hd
    group = n_heads // ATT_KV_HEADS
    pairs = group // 2
    first = (pl.program_id(0) == 0) & (pl.program_id(1) == 0)

    @pl.when(first)
    def _():
        bias_ref[...] = jnp.full(bias_ref.shape, -jnp.inf, F32)
        bucket = bucket_ref[...]

        def fill(b, carry):
            hit = bucket == b
            for hh in range(n_heads):
                bias_ref[hh] = jnp.where(hit, relb_ref[b, hh] * LOG2E, bias_ref[hh])
            return carry

        lax.fori_loop(0, REL_BUCKETS, fill, 0)

    h = h_ref[...]
    u = _rms_scale(h, g_ref[...]).astype(BF16)
    q = jnp.dot(u, wq_ref[...].astype(BF16), preferred_element_type=F32)
    lane_lo = _pair_lane_mask()

    q16 = jnp.concatenate(
        [(q[:, p * pw:(p + 1) * pw]
          * _pair_rms_scale(q[:, p * pw:(p + 1) * pw], lane_lo, hd ** -0.5 * LOG2E)).astype(BF16)
         for p in range(n_heads // 2)], axis=1)

    kall = jnp.concatenate([kp_ref[...], kc_ref[...]], axis=0).astype(F32) * qn_ref[...]
    vall = jnp.concatenate([vp_ref[...], vc_ref[...]], axis=0).astype(F32)
    kswap = pltpu.roll(kall, hd, 1)
    vswap = pltpu.roll(vall, hd, 1)
    k_slots = []
    v_slots = []
    for kvh in range(ATT_KV_HEADS):
        src_k, alt_k = (kall, kswap) if kvh == 0 else (kswap, kall)
        src_v, alt_v = (vall, vswap) if kvh == 0 else (vswap, vall)
        k_slots.append((jnp.where(lane_lo, src_k, 0.0).astype(BF16), jnp.where(lane_lo, 0.0, alt_k).astype(BF16)))
        v_slots.append((jnp.where(lane_lo, src_v, 0.0).astype(BF16), jnp.where(lane_lo, 0.0, alt_v).astype(BF16)))

    kj = lax.broadcasted_iota(jnp.int32, (1, 2 * blk), 1)
    key_ok = kj >= jnp.where(pl.program_id(1) > 0, 0, blk)

    outs = {}
    for kvh in range(ATT_KV_HEADS):
        for r in range(tq // blk):
            qrows = slice(r * blk, (r + 1) * blk)
            krows = slice(r * blk, r * blk + 2 * blk)
            out_pairs = []
            q_stack = jnp.concatenate(
                [q16[qrows, (kvh * pairs + j) * pw:(kvh * pairs + j + 1) * pw] for j in range(pairs)],
                axis=0)
            s_slots = [lax.dot_general(q_stack, ks[krows, :], (((1,), (1,)), ((), ())),
                                       preferred_element_type=F32) for ks in k_slots[kvh]]
            v_stack = jnp.concatenate([vs[krows, :] for vs in v_slots[kvh]], axis=0)
            p_rows = []
            scales = []
            for j in range(pairs):
                probs = []
                rinv = []
                for e in range(2):
                    hh = kvh * group + 2 * j + e
                    s = s_slots[e][j * blk:(j + 1) * blk, :] + bias_ref[hh]
                    if r == 0:
                        s = jnp.where(key_ok, s, -jnp.inf)
                    sink = sink_ref[hh] * LOG2E
                    m = jnp.maximum(jnp.max(s, axis=-1, keepdims=True), sink)
                    pexp = jnp.exp2(s - m)
                    denom = jnp.sum(pexp, axis=-1, keepdims=True) + jnp.exp2(sink - m)
                    probs.append(pexp.astype(BF16))
                    rinv.append(1.0 / denom)
                p_rows.append(jnp.concatenate(probs, axis=1))
                scales.append(jnp.where(lane_lo, rinv[0], rinv[1]))
            pv = jnp.dot(jnp.concatenate(p_rows, axis=0), v_stack, preferred_element_type=F32)
            for j in range(pairs):
                out_pairs.append((pv[j * blk:(j + 1) * blk, :] * scales[j]).astype(BF16))
            outs[(r, kvh)] = jnp.concatenate(out_pairs, axis=1)
    o_rows = [jnp.concatenate([outs[(r, k)] for k in range(ATT_KV_HEADS)], axis=1) for r in range(tq // blk)]
    o = jnp.concatenate(o_rows, axis=0)
    o_ref[...] = h + jnp.dot(o, wo_ref[...].astype(BF16), preferred_element_type=F32)


def _attention_block(h3, k, v, norm_g, w_q, q_norm, sinks, rel_bias, w_o):
    bsz, t, d = h3.shape
    n_heads = sinks.shape[0]
    hd = ATT_HEAD_DIM
    kw = ATT_KV_HEADS * hd
    blk = ATT_WINDOW
    assert w_q.shape == (d, n_heads * hd) and w_o.shape == (n_heads * hd, d)
    assert n_heads % (2 * ATT_KV_HEADS) == 0 and kw == V7X_LANES
    tq = min(ATT_TOKEN_TILE, t)
    assert t % tq == 0 and tq % blk == 0
    ratio = tq // blk
    bucket = _t5_bucket_table(blk)
    qn = jnp.tile(q_norm.astype(F32), ATT_KV_HEADS).reshape(1, kw)
    k3 = k.reshape(bsz, t, kw)
    v3 = v.reshape(bsz, t, kw)

    def prev_blk(b, i, *_):
        return (b, jnp.maximum(i * ratio - 1, 0), 0)

    def cur_blk(b, i, *_):
        return (b, i, 0)

    def const2(b, i, *_):
        return (0, 0)

    one = pl.Buffered(1)
    grid_spec = pltpu.PrefetchScalarGridSpec(
        num_scalar_prefetch=2,
        grid=(bsz, t // tq),
        in_specs=[
            pl.BlockSpec((None, tq, d), cur_blk),
            pl.BlockSpec((1, d), const2, pipeline_mode=one),
            pl.BlockSpec((d, n_heads * hd), const2, pipeline_mode=one),
            pl.BlockSpec((1, kw), const2, pipeline_mode=one),
            pl.BlockSpec((None, blk, kw), prev_blk),
            pl.BlockSpec((None, tq, kw), cur_blk),
            pl.BlockSpec((None, blk, kw), prev_blk),
            pl.BlockSpec((None, tq, kw), cur_blk),
            pl.BlockSpec((blk, 2 * blk), const2, pipeline_mode=one),
            pl.BlockSpec((n_heads * hd, d), const2, pipeline_mode=one),
        ],
        out_specs=pl.BlockSpec((None, tq, d), cur_blk),
        scratch_shapes=[pltpu.VMEM((n_heads, blk, 2 * blk), F32)],
    )
    return pl.pallas_call(
        functools.partial(_attn_kernel, n_heads=n_heads),
        out_shape=jax.ShapeDtypeStruct((bsz, t, d), F32),
        grid_spec=grid_spec,
        compiler_params=pltpu.CompilerParams(
            dimension_semantics=("arbitrary", "arbitrary"),
            vmem_limit_bytes=V7X_VMEM_LIMIT_BYTES),
        name="swa_block",
    )(rel_bias.astype(F32), sinks.astype(F32), h3, norm_g.reshape(1, d), w_q.astype(F32), qn,
      k3, k3, v3, v3, bucket, w_o.astype(F32))


def kernel(x, ffn_norm, ffn_w1, ffn_w3, ffn_w2, ssm_norm, ssm_w_in, ssm_conv_w, ssm_conv_b, ssm_dt_bias, ssm_a_log, ssm_d, ssm_gate_norm, ssm_w_out, kv_norm, w_kv, k_norm, attn_norm, w_q, q_norm, sinks, w_o, rel_bias):
    bsz, t, d = x.shape
    depth = ffn_norm.shape[0]
    n_a = ssm_norm.shape[0]
    n = bsz * t
    w1 = ffn_w1.astype(F32)
    w3 = ffn_w3.astype(F32)
    w2 = ffn_w2.astype(F32)

    h = x
    k_shared = v_shared = None
    for layer in range(depth):
        h = _ffn_half_step(h.reshape(n, d), ffn_norm[layer, 0], w1, w3, w2, layer, 0).reshape(bsz, t, d)
        if layer < n_a:
            i = layer
            h = _mamba_block(h, ssm_norm[i], ssm_w_in[i], ssm_conv_w[i], ssm_conv_b[i], ssm_dt_bias[i],
                             ssm_a_log[i], ssm_d[i], ssm_gate_norm[i], ssm_w_out[i])
        else:
            j = layer - n_a
            h = _attention_block(h, k_shared, v_shared, attn_norm[j], w_q[j], q_norm[j], sinks[j],
                                 rel_bias, w_o[j])
        if layer == n_a - 1:
            h2, k2, v2 = _ffn_half_step(h.reshape(n, d), ffn_norm[layer, 1], w1, w3, w2, layer, 1,
                                        kv_params=(kv_norm, w_kv.astype(BF16), k_norm))
            k_shared, v_shared = k2, v2
        else:
            h2 = _ffn_half_step(h.reshape(n, d), ffn_norm[layer, 1], w1, w3, w2, layer, 1)
        h = h2.reshape(bsz, t, d)
    return h
```

```python
import functools
import math

import jax
import jax.numpy as jnp
from jax import lax
from jax.experimental import pallas as pl
from jax.experimental.pallas import tpu as pltpu

F32 = jnp.float32
BF16 = jnp.bfloat16

EPS = 1e-6
FFN_HALF = 0.5
LOG2E = 1.4426950408889634

SSM_HEAD_DIM = 64
SSM_GROUPS = 4
SSM_STATE = 128
SSM_CONV = 4
ATT_HEAD_DIM = 64
ATT_KV_HEADS = 2
ATT_WINDOW = 128
REL_BUCKETS = 32

V7X_LANES = 128
V7X_SUBLANES = 8
V7X_VMEM_LIMIT_BYTES = 56 * 1024 * 1024

FFN_TOKEN_TILE = 1024
FFN_CHUNK = 256
SSM_TOKEN_TILE = 256
SSM_SUBCHUNK = 128
ATT_TOKEN_TILE = 512


def _rms_scale(x, g):
    ms = jnp.mean(x * x, axis=-1, keepdims=True)
    return x * lax.rsqrt(ms + EPS) * g


def _silu(x):
    hx = 0.5 * x
    return hx + hx * jnp.tanh(hx)


def _const_spec(shape):
    zeros = (0,) * len(shape)
    return pl.BlockSpec(shape, lambda *_: zeros, pipeline_mode=pl.Buffered(1))


def _pair_lane_mask():
    return lax.broadcasted_iota(jnp.int32, (1, 2 * ATT_HEAD_DIM), 1) < ATT_HEAD_DIM


def _pair_rms_scale(x, lane_lo, post):
    xsq = x * x
    s_lo = jnp.sum(jnp.where(lane_lo, xsq, 0.0), axis=-1, keepdims=True)
    s_hi = jnp.sum(jnp.where(lane_lo, 0.0, xsq), axis=-1, keepdims=True)
    inv = 1.0 / ATT_HEAD_DIM
    return jnp.where(lane_lo, lax.rsqrt(s_lo * inv + EPS) * post, lax.rsqrt(s_hi * inv + EPS) * post)


def _ffn_kernel(*refs, n_chunks, fc, with_kv):
    if with_kv:
        h_ref, g_ref, w1_ref, w3_ref, w2_ref, kvg_ref, wkv_ref, kn_ref, o_ref, k_ref, v_ref = refs
    else:
        h_ref, g_ref, w1_ref, w3_ref, w2_ref, o_ref = refs
    h = h_ref[...]
    xn = h * lax.rsqrt(jnp.mean(h * h, axis=-1, keepdims=True) + EPS)
    u = (xn * g_ref[...]).astype(BF16)
    if with_kv:
        kw = k_ref.shape[1]
        kv = jnp.dot((xn * kvg_ref[...]).astype(BF16), wkv_ref[...], preferred_element_type=F32)
        k = kv[:, :kw]
        k_ref[...] = (k * _pair_rms_scale(k, _pair_lane_mask(), 1.0) * kn_ref[...]).astype(k_ref.dtype)
        v_ref[...] = kv[:, kw:].astype(v_ref.dtype)
    for c in range(n_chunks):
        cols = slice(c * fc, (c + 1) * fc)
        a = jnp.dot(u, w1_ref[:, cols].astype(BF16), preferred_element_type=F32)
        b = jnp.dot(u, w3_ref[:, cols].astype(BF16), preferred_element_type=F32)
        gate = (_silu(a) * b).astype(BF16)
        d = jnp.dot(gate, w2_ref[cols, :].astype(BF16), preferred_element_type=F32)
        if c == 0:
            o_ref[...] = d
        else:
            o_ref[...] += d
    o_ref[...] = h + FFN_HALF * o_ref[...]


def _ffn_half_step(h, g, w1, w3, w2, layer, half, kv_params=None):
    n, d = h.shape
    f = w1.shape[-1]
    tm = min(FFN_TOKEN_TILE, n)
    fc = FFN_CHUNK
    assert n % tm == 0 and f % fc == 0
    with_kv = kv_params is not None

    def wsel(*_):
        return (layer, half, 0, 0)

    one = pl.Buffered(1)
    row = pl.BlockSpec((tm, d), lambda i: (i, 0))
    in_specs = [
        row,
        _const_spec((1, d)),
        pl.BlockSpec((None, None, d, f), wsel, pipeline_mode=one),
        pl.BlockSpec((None, None, d, f), wsel, pipeline_mode=one),
        pl.BlockSpec((None, None, f, d), wsel, pipeline_mode=one),
    ]
    args = [h, g.reshape(1, d), w1, w3, w2]
    out_shape = jax.ShapeDtypeStruct((n, d), F32)
    out_specs = row
    if with_kv:
        kv_norm, w_kv16, k_norm = kv_params
        kw = ATT_KV_HEADS * ATT_HEAD_DIM
        assert w_kv16.shape == (d, 2 * kw) and kw == V7X_LANES
        kn = jnp.tile(k_norm.astype(F32), ATT_KV_HEADS).reshape(1, kw)
        in_specs += [_const_spec((1, d)), _const_spec((d, 2 * kw)), _const_spec((1, kw))]
        args += [kv_norm.reshape(1, d), w_kv16, kn]
        kv_spec = pl.BlockSpec((tm, kw), lambda i: (i, 0))
        out_shape = (out_shape, jax.ShapeDtypeStruct((n, kw), BF16), jax.ShapeDtypeStruct((n, kw), BF16))
        out_specs = (row, kv_spec, kv_spec)
    return pl.pallas_call(
        functools.partial(_ffn_kernel, n_chunks=f // fc, fc=fc, with_kv=with_kv),
        out_shape=out_shape,
        grid=(n // tm,),
        in_specs=in_specs,
        out_specs=out_specs,
        compiler_params=pltpu.CompilerParams(
            dimension_semantics=("arbitrary",), vmem_limit_bytes=V7X_VMEM_LIMIT_BYTES),
        name="ffn_kv_half_step" if with_kv else "ffn_half_step",
    )(*args)


N_HANDOVER = 8


def _pair_cols(v, r0, lane_lo):
    if v.shape[0] == 1:
        return jnp.where(lane_lo, v[:, r0:r0 + 1], v[:, r0 + 1:r0 + 2])
    idx = jnp.where(jnp.broadcast_to(lane_lo, v.shape), r0, r0 + 1)
    return jnp.take_along_axis(v, idx, axis=1)


def _bf16_terms(x):
    hi = x.astype(BF16)
    r1 = x - hi.astype(F32)
    mid = r1.astype(BF16)
    lo = (r1 - mid.astype(F32)).astype(BF16)
    return hi, mid, lo


def _exact_f32_dot(a, b, split_rhs):
    if split_rhs:
        n = b.shape[1]
        out = jnp.dot(a, jnp.concatenate(_bf16_terms(b), axis=1), preferred_element_type=F32)
        return out[:, :n] + out[:, n:2 * n] + out[:, 2 * n:]
    m = a.shape[0]
    out = jnp.dot(jnp.concatenate(_bf16_terms(a), axis=0), b, preferred_element_type=F32)
    return out[:m] + out[m:2 * m] + out[2 * m:]


def _mamba_stage_a(h_ref, g_ref, win_ref, wdt_ref, wdtT_ref, cw_ref, cb_ref,
                   dtb_ref, dtbT_ref, alog_ref, alogT_ref, xpad_ref, bufs, *, n_heads):
    hs_ref, z_ref, xs_ref, b_ref, c_ref, dt_ref, da_ref, daT_ref = bufs
    L = h_ref.shape[0]
    d_inner = z_ref.shape[1]
    pw = 2 * SSM_HEAD_DIM
    n_slab = xpad_ref.shape[0]
    n_xs = xs_ref.shape[0]
    pad = V7X_SUBLANES
    zc = 4 * pw

    h = h_ref[...]
    hs_ref[...] = h
    u = _rms_scale(h, g_ref[...]).astype(BF16)
    yield

    def softplus(x):
        return jnp.maximum(x, 0.0) + jnp.log1p(jnp.exp(-jnp.abs(x)))

    x0 = d_inner
    dt_raw = jnp.dot(u, wdt_ref[...], preferred_element_type=F32)
    dt_rawT = lax.dot_general(wdtT_ref[...], u, (((1,), (1,)), ((), ())),
                              preferred_element_type=F32)
    dt = softplus(dt_raw + dtb_ref[...])
    dtT = softplus(dt_rawT + dtbT_ref[...])
    dt_ref[...] = dt
    da_ref[...] = dt * (-jnp.exp(alog_ref[...]))
    daT_ref[...] = dtT * (-jnp.exp(alogT_ref[...]))
    yield

    def x_unit(j0):
        xj = jnp.dot(u, win_ref[:, x0 + j0 * pw:x0 + (j0 + 2) * pw], preferred_element_type=F32)
        xpad_ref[j0, pad:pad + L, :] = xj[:, :pw]
        xpad_ref[j0 + 1, pad:pad + L, :] = xj[:, pw:]
        for j in (j0, j0 + 1):
            acc = cb_ref[:, j * pw:(j + 1) * pw]
            for k in range(SSM_CONV):
                off = pad - (SSM_CONV - 1) + k
                acc = acc + cw_ref[k:k + 1, j * pw:(j + 1) * pw] * xpad_ref[j, off:off + L, :]
            xpad_ref[j, 0:pad, :] = xpad_ref[j, L:L + pad, :]
            act = _silu(acc)
            if j < n_xs:
                xs_ref[j] = act
            elif j < n_xs + SSM_GROUPS:
                b_ref[j - n_xs] = act.astype(BF16)
            else:
                c_ref[j - n_xs - SSM_GROUPS] = act.astype(BF16)

    def z_unit(c0):
        z_ref[:, c0:c0 + zc] = jnp.dot(u, win_ref[:, c0:c0 + zc], preferred_element_type=F32)

    x_units = [functools.partial(x_unit, j0) for j0 in range(0, n_slab, 2)]
    z_units = [functools.partial(z_unit, c0) for c0 in range(0, d_inner, zc)]
    order = []
    per = len(x_units) // len(z_units)
    for i, zu in enumerate(z_units):
        order += x_units[i * per:(i + 1) * per] + [zu]
    order += x_units[len(z_units) * per:]
    for unit in order:
        unit()
        yield


def _mamba_stage_b(bufs, dskip_ref, gn_ref, wo_ref, o_ref, state_ref, y_ref, *, n_heads):
    hs_ref, z_ref, xs_ref, b_ref, c_ref, dt_ref, da_ref, daT_ref = bufs
    L = hs_ref.shape[0]
    ls = SSM_SUBCHUNK
    d_inner = z_ref.shape[1]
    hp = SSM_HEAD_DIM
    pw = 2 * hp
    pairs_per_group = n_heads // SSM_GROUPS // 2

    ri = lax.broadcasted_iota(jnp.int32, (ls, ls), 0)
    ci = lax.broadcasted_iota(jnp.int32, (ls, ls), 1)
    causal = ri >= ci
    tri = jnp.where(causal, 1.0, 0.0).astype(BF16)
    triT = jnp.where(ci >= ri, 1.0, 0.0).astype(BF16)
    lane_lo = lax.broadcasted_iota(jnp.int32, (1, pw), 1) < hp
    pending = []
    state_todo = []

    def update_state():
        while state_todo:
            g_, bg_, xw_, el_ = state_todo.pop(0)
            inc = lax.dot_general(bg_, xw_, (((0,), (0,)), ((), ())),
                                  preferred_element_type=F32)
            state_ref[g_] = state_ref[g_] * el_ + inc

    def flush(keep=0):
        while len(pending) > keep:
            lhs, rhs_, off, rows_, p_ = pending.pop(0)
            y_ref[rows_, p_ * pw:(p_ + 1) * pw] = jnp.dot(lhs, rhs_, preferred_element_type=F32) + off

    for sc in range(L // ls):
        rows = slice(sc * ls, (sc + 1) * ls)
        acum = _exact_f32_dot(tri, da_ref[rows, :], split_rhs=True)
        acumT = _exact_f32_dot(daT_ref[:, rows], triT, split_rhs=False)
        a_last = acum[ls - 1:ls, :]
        e_acum = jnp.exp(acum)
        e_last = jnp.exp(a_last)
        dt_s = dt_ref[rows, :]
        w_end = jnp.exp(a_last - acum) * dt_s
        acum2 = acum * LOG2E
        acumT2 = acumT * LOG2E
        update_state()
        cbs = []
        y_offs = []
        for g in range(SSM_GROUPS):
            cg = c_ref[g, rows, :]
            cbs.append(lax.dot_general(cg, b_ref[g, rows, :], (((1,), (1,)), ((), ())),
                                       preferred_element_type=F32).astype(BF16))
            y_offs.append(jnp.dot(cg, state_ref[g].astype(BF16), preferred_element_type=F32))
        yield
        for g in range(SSM_GROUPS):
            bg = b_ref[g, rows, :]
            cb16 = cbs[g]
            y_off = y_offs[g]
            xw_parts = []
            el_parts = []
            for j in range(pairs_per_group):
                p = g * pairs_per_group + j
                r0 = 2 * p
                xs_pair = xs_ref[p, rows, :]
                xd = xs_pair * _pair_cols(dt_s, r0, lane_lo)
                rhs = jnp.concatenate([jnp.where(lane_lo, xd, 0.0), jnp.where(lane_lo, 0.0, xd)],
                                      axis=0).astype(BF16)
                ms = []
                for r in (r0, r0 + 1):
                    seg = acum2[:, r:r + 1] - acumT2[r:r + 1, :]
                    dec = jnp.exp2(jnp.where(causal, seg, -jnp.inf))
                    ms.append(cb16 * dec.astype(BF16))
                flush(keep=2)
                pending.append((jnp.concatenate(ms, axis=1), rhs,
                                y_off[:, j * pw:(j + 1) * pw] * _pair_cols(e_acum, r0, lane_lo), rows, p))
                xw_parts.append((xs_pair * _pair_cols(w_end, r0, lane_lo)).astype(BF16))
                el_parts.append(_pair_cols(e_last, r0, lane_lo))
                if j % 2 == 1 and j + 1 < pairs_per_group:
                    yield
            xw = jnp.concatenate(xw_parts, axis=1)
            el = jnp.concatenate(el_parts, axis=1)
            update_state()
            state_todo.append((g, bg, xw, el))
            yield

    flush()
    update_state()
    gw = d_inner // SSM_GROUPS
    slabs_per_group = gw // pw
    for g in range(SSM_GROUPS):
        cols = slice(g * gw, (g + 1) * gw)
        xs_g = jnp.concatenate([xs_ref[g * slabs_per_group + i] for i in range(slabs_per_group)], axis=1)
        yg = (y_ref[:, cols] + dskip_ref[:, cols] * xs_g) * _silu(z_ref[:, cols])
        yn = _rms_scale(yg, gn_ref[:, cols]).astype(BF16)
        part = jnp.dot(yn, wo_ref[cols, :], preferred_element_type=F32)
        if g == 0:
            o_ref[...] = hs_ref[...] + part
        else:
            o_ref[...] += part
        yield


def _interleave(*gens):
    live = list(gens)
    while live:
        for gen in list(live):
            try:
                next(gen)
            except StopIteration:
                live.remove(gen)


def _mamba_kernel(h_ref, g_ref, win_ref, wdt_ref, wdtT_ref, cw_ref, cb_ref,
                  dtb_ref, dtbT_ref, alog_ref, alogT_ref, dskip_ref, gn_ref, wo_ref,
                  o_ref, xpad_ref, state_ref, y_ref, *handover, n_heads, tiles_per_seq):
    slots = (handover[:N_HANDOVER], handover[N_HANDOVER:])
    s = pl.program_id(0)
    pad = V7X_SUBLANES

    @pl.when(s == 0)
    def _():
        for ref in slots[1]:
            ref[...] = jnp.zeros_like(ref)

    @pl.when(lax.rem(s, tiles_per_seq) == 0)
    def _():
        xpad_ref[:, 0:pad, :] = jnp.zeros((xpad_ref.shape[0], pad, xpad_ref.shape[2]), F32)

    @pl.when((s == 0) | (lax.rem(s - 1, tiles_per_seq) == 0))
    def _():
        state_ref[...] = jnp.zeros_like(state_ref)

    def step(write_slot):
        _interleave(
            _mamba_stage_b(slots[1 - write_slot], dskip_ref, gn_ref, wo_ref, o_ref, state_ref, y_ref,
                           n_heads=n_heads),
            _mamba_stage_a(h_ref, g_ref, win_ref, wdt_ref, wdtT_ref, cw_ref, cb_ref,
                           dtb_ref, dtbT_ref, alog_ref, alogT_ref, xpad_ref, slots[write_slot],
                           n_heads=n_heads))

    parity = lax.rem(s, 2)

    @pl.when(parity == 0)
    def _():
        step(0)

    @pl.when(parity == 1)
    def _():
        step(1)


def _mamba_block(h3, norm_g, w_in, conv_w, conv_b, dt_bias, a_log, d_skip, gate_norm, w_out):
    bsz, t, d = h3.shape
    n_heads = dt_bias.shape[0]
    d_inner = n_heads * SSM_HEAD_DIM
    conv_dim = d_inner + 2 * SSM_GROUPS * SSM_STATE
    pw = 2 * SSM_HEAD_DIM
    assert w_in.shape == (d, d_inner + conv_dim + n_heads)
    assert n_heads % (2 * SSM_GROUPS) == 0 and n_heads <= V7X_LANES
    assert pw == V7X_LANES and SSM_STATE == V7X_LANES and (conv_dim // pw) % 2 == 0
    L = SSM_TOKEN_TILE
    assert t % L == 0 and L % SSM_SUBCHUNK == 0
    hpg = n_heads // SSM_GROUPS
    tiles_per_seq = t // L
    n_tiles = bsz * tiles_per_seq

    w_in16 = w_in.astype(BF16)
    wdt = w_in16[:, d_inner + conv_dim:]
    wdtT = wdt.T
    lane_pad = V7X_LANES - n_heads
    wdt_pad = jnp.pad(wdt, ((0, 0), (0, lane_pad)))
    dskip_row = jnp.repeat(d_skip.astype(F32), SSM_HEAD_DIM).reshape(1, d_inner)

    arrays = [
        norm_g.reshape(1, d),
        w_in16, wdt_pad, wdtT,
        conv_w.astype(F32), conv_b.reshape(1, conv_dim).astype(F32),
        jnp.pad(dt_bias.reshape(1, n_heads).astype(F32), ((0, 0), (0, lane_pad))),
        dt_bias.reshape(n_heads, 1).astype(F32),
        jnp.pad(a_log.reshape(1, n_heads).astype(F32), ((0, 0), (0, lane_pad))),
        a_log.reshape(n_heads, 1).astype(F32),
        dskip_row,
        gate_norm.reshape(1, d_inner).astype(F32),
        w_out.astype(BF16),
    ]
    handover = [
        pltpu.VMEM((L, d), F32),
        pltpu.VMEM((L, d_inner), F32),
        pltpu.VMEM((d_inner // pw, L, pw), F32),
        pltpu.VMEM((SSM_GROUPS, L, SSM_STATE), BF16),
        pltpu.VMEM((SSM_GROUPS, L, SSM_STATE), BF16),
        pltpu.VMEM((L, V7X_LANES), F32),
        pltpu.VMEM((L, V7X_LANES), F32),
        pltpu.VMEM((n_heads, L), F32),
    ]
    assert len(handover) == N_HANDOVER
    in_specs = [pl.BlockSpec((L, d), lambda s: (jnp.minimum(s, n_tiles - 1), 0))]
    in_specs += [_const_spec(a.shape) for a in arrays]
    out = pl.pallas_call(
        functools.partial(_mamba_kernel, n_heads=n_heads, tiles_per_seq=tiles_per_seq),
        out_shape=jax.ShapeDtypeStruct((bsz * t, d), F32),
        grid=(n_tiles + 1,),
        in_specs=in_specs,
        out_specs=pl.BlockSpec((L, d), lambda s: (jnp.maximum(s - 1, 0), 0)),
        scratch_shapes=[
            pltpu.VMEM((conv_dim // pw, L + V7X_SUBLANES, pw), F32),
            pltpu.VMEM((SSM_GROUPS, SSM_STATE, hpg * SSM_HEAD_DIM), F32),
            pltpu.VMEM((L, d_inner), F32),
        ] + handover + handover,
        compiler_params=pltpu.CompilerParams(
            dimension_semantics=("arbitrary",),
            vmem_limit_bytes=V7X_VMEM_LIMIT_BYTES),
        name="mamba2_block",
    )(h3.reshape(bsz * t, d), *arrays)
    return out.reshape(bsz, t, d)


def _t5_bucket_table(blk):
    qi = jnp.arange(blk)[:, None] + blk
    kj = jnp.arange(2 * blk)[None, :]
    dist = qi - kj
    n = jnp.maximum(dist, 0)
    max_exact = REL_BUCKETS // 2
    nf = jnp.maximum(n, 1).astype(F32)
    large = max_exact + (jnp.log(nf / max_exact) / math.log(ATT_WINDOW / max_exact)
                         * (REL_BUCKETS - max_exact)).astype(jnp.int32)
    large = jnp.minimum(large, REL_BUCKETS - 1)
    bucket = jnp.where(n < max_exact, n, large)
    in_window = (dist >= 0) & (dist < ATT_WINDOW)
    return jnp.where(in_window, bucket, REL_BUCKETS).astype(jnp.int32)


def _attn_kernel(sink_ref, h_ref, g_ref, wq_ref, qn_ref, kp_ref, kc_ref, vp_ref, vc_ref,
                 bucket_ref, relb_ref, wo_ref, o_ref, bias_ref, *, n_heads):
    tq = h_ref.shape[0]
    blk = ATT_WINDOW
    hd = ATT_HEAD_DIM
    pw = 2 * hd
    group = n_heads // ATT_KV_HEADS
    pairs = group // 2
    first = (pl.program_id(0) == 0) & (pl.program_id(1) == 0)

    @pl.when(first)
    def _():
        bucket = bucket_ref[...]
        for hh in range(n_heads):
            row = jnp.broadcast_to(relb_ref[hh:hh + 1, :] * LOG2E, (blk, V7X_LANES))
            for c0 in range(0, 2 * blk, V7X_LANES):
                bias_ref[hh, :, c0:c0 + V7X_LANES] = jnp.take_along_axis(
                    row, bucket[:, c0:c0 + V7X_LANES], axis=1)

    h = h_ref[...]
    u = _rms_scale(h, g_ref[...]).astype(BF16)
    q = jnp.dot(u, wq_ref[...].astype(BF16), preferred_element_type=F32)
    lane_lo = _pair_lane_mask()

    q16 = jnp.concatenate(
        [(q[:, p * pw:(p + 1) * pw]
          * _pair_rms_scale(q[:, p * pw:(p + 1) * pw], lane_lo, hd ** -0.5 * LOG2E)).astype(BF16)
         for p in range(n_heads // 2)], axis=1)

    kall = jnp.concatenate([kp_ref[...], kc_ref[...]], axis=0).astype(F32) * qn_ref[...]
    vall = jnp.concatenate([vp_ref[...], vc_ref[...]], axis=0).astype(F32)
    kswap = pltpu.roll(kall, hd, 1)
    vswap = pltpu.roll(vall, hd, 1)
    k_slots = []
    v_slots = []
    for kvh in range(ATT_KV_HEADS):
        src_k, alt_k = (kall, kswap) if kvh == 0 else (kswap, kall)
        src_v, alt_v = (vall, vswap) if kvh == 0 else (vswap, vall)
        k_slots.append((jnp.where(lane_lo, src_k, 0.0).astype(BF16), jnp.where(lane_lo, 0.0, alt_k).astype(BF16)))
        v_slots.append((jnp.where(lane_lo, src_v, 0.0).astype(BF16), jnp.where(lane_lo, 0.0, alt_v).astype(BF16)))

    kj = lax.broadcasted_iota(jnp.int32, (1, 2 * blk), 1)
    key_ok = kj >= jnp.where(pl.program_id(1) > 0, 0, blk)

    outs = {}
    for kvh in range(ATT_KV_HEADS):
        for r in range(tq // blk):
            qrows = slice(r * blk, (r + 1) * blk)
            krows = slice(r * blk, r * blk + 2 * blk)
            out_pairs = []
            q_stack = jnp.concatenate(
                [q16[qrows, (kvh * pairs + j) * pw:(kvh * pairs + j + 1) * pw] for j in range(pairs)],
                axis=0)
            s_slots = [lax.dot_general(q_stack, ks[krows, :], (((1,), (1,)), ((), ())),
                                       preferred_element_type=F32) for ks in k_slots[kvh]]
            v_stack = jnp.concatenate([vs[krows, :] for vs in v_slots[kvh]], axis=0)
            p_rows = []
            scales = []
            for j in range(pairs):
                probs = []
                rinv = []
                for e in range(2):
                    hh = kvh * group + 2 * j + e
                    s = s_slots[e][j * blk:(j + 1) * blk, :] + bias_ref[hh]
                    if r == 0:
                        s = jnp.where(key_ok, s, -jnp.inf)
                    sink = sink_ref[hh] * LOG2E
                    m = jnp.maximum(jnp.max(s, axis=-1, keepdims=True), sink)
                    pexp = jnp.exp2(s - m)
                    denom = jnp.sum(pexp, axis=-1, keepdims=True) + jnp.exp2(sink - m)
                    probs.append(pexp.astype(BF16))
                    rinv.append(1.0 / denom)
                p_rows.append(jnp.concatenate(probs, axis=1))
                scales.append(jnp.where(lane_lo, rinv[0], rinv[1]))
            pv = jnp.dot(jnp.concatenate(p_rows, axis=0), v_stack, preferred_element_type=F32)
            for j in range(pairs):
                out_pairs.append((pv[j * blk:(j + 1) * blk, :] * scales[j]).astype(BF16))
            outs[(r, kvh)] = jnp.concatenate(out_pairs, axis=1)
    o_rows = [jnp.concatenate([outs[(r, k)] for k in range(ATT_KV_HEADS)], axis=1) for r in range(tq // blk)]
    o = jnp.concatenate(o_rows, axis=0)
    o_ref[...] = h + jnp.dot(o, wo_ref[...].astype(BF16), preferred_element_type=F32)


def _attention_block(h3, k, v, norm_g, w_q, q_norm, sinks, rel_bias, w_o):
    bsz, t, d = h3.shape
    n_heads = sinks.shape[0]
    hd = ATT_HEAD_DIM
    kw = ATT_KV_HEADS * hd
    blk = ATT_WINDOW
    assert w_q.shape == (d, n_heads * hd) and w_o.shape == (n_heads * hd, d)
    assert n_heads % (2 * ATT_KV_HEADS) == 0 and kw == V7X_LANES
    tq = min(ATT_TOKEN_TILE, t)
    assert t % tq == 0 and tq % blk == 0
    ratio = tq // blk
    bucket = _t5_bucket_table(blk)
    qn = jnp.tile(q_norm.astype(F32), ATT_KV_HEADS).reshape(1, kw)
    assert rel_bias.shape == (REL_BUCKETS, n_heads) and REL_BUCKETS < V7X_LANES
    relb_rows = jnp.pad(rel_bias.T.astype(F32), ((0, 0), (0, V7X_LANES - REL_BUCKETS)), constant_values=-jnp.inf)
    k3 = k.reshape(bsz, t, kw)
    v3 = v.reshape(bsz, t, kw)

    def prev_blk(b, i, *_):
        return (b, jnp.maximum(i * ratio - 1, 0), 0)

    def cur_blk(b, i, *_):
        return (b, i, 0)

    def const2(b, i, *_):
        return (0, 0)

    one = pl.Buffered(1)
    grid_spec = pltpu.PrefetchScalarGridSpec(
        num_scalar_prefetch=1,
        grid=(bsz, t // tq),
        in_specs=[
            pl.BlockSpec((None, tq, d), cur_blk),
            pl.BlockSpec((1, d), const2, pipeline_mode=one),
            pl.BlockSpec((d, n_heads * hd), const2, pipeline_mode=one),
            pl.BlockSpec((1, kw), const2, pipeline_mode=one),
            pl.BlockSpec((None, blk, kw), prev_blk),
            pl.BlockSpec((None, tq, kw), cur_blk),
            pl.BlockSpec((None, blk, kw), prev_blk),
            pl.BlockSpec((None, tq, kw), cur_blk),
            pl.BlockSpec((blk, 2 * blk), const2, pipeline_mode=one),
            pl.BlockSpec((n_heads, V7X_LANES), const2, pipeline_mode=one),
            pl.BlockSpec((n_heads * hd, d), const2, pipeline_mode=one),
        ],
        out_specs=pl.BlockSpec((None, tq, d), cur_blk),
        scratch_shapes=[pltpu.VMEM((n_heads, blk, 2 * blk), F32)],
    )
    return pl.pallas_call(
        functools.partial(_attn_kernel, n_heads=n_heads),
        out_shape=jax.ShapeDtypeStruct((bsz, t, d), F32),
        grid_spec=grid_spec,
        compiler_params=pltpu.CompilerParams(
            dimension_semantics=("arbitrary", "arbitrary"),
            vmem_limit_bytes=V7X_VMEM_LIMIT_BYTES),
        name="swa_block",
    )(sinks.astype(F32), h3, norm_g.reshape(1, d), w_q.astype(F32), qn,
      k3, k3, v3, v3, bucket, relb_rows, w_o.astype(F32))


def kernel(x, ffn_norm, ffn_w1, ffn_w3, ffn_w2, ssm_norm, ssm_w_in, ssm_conv_w, ssm_conv_b, ssm_dt_bias, ssm_a_log, ssm_d, ssm_gate_norm, ssm_w_out, kv_norm, w_kv, k_norm, attn_norm, w_q, q_norm, sinks, w_o, rel_bias):
    bsz, t, d = x.shape
    depth = ffn_norm.shape[0]
    n_a = ssm_norm.shape[0]
    n = bsz * t
    w1 = ffn_w1.astype(F32)
    w3 = ffn_w3.astype(F32)
    w2 = ffn_w2.astype(F32)

    h = x
    k_shared = v_shared = None
    for layer in range(depth):
        if layer == n_a and layer > 0:
            h, k_shared, v_shared = _ffn_half_step(h.reshape(n, d), ffn_norm[layer, 0], w1, w3, w2, layer, 0,
                                                   kv_params=(kv_norm, w_kv.astype(BF16), k_norm))
        else:
            h = _ffn_half_step(h.reshape(n, d), ffn_norm[layer, 0], w1, w3, w2, layer, 0)
        h = h.reshape(bsz, t, d)
        if layer < n_a:
            i = layer
            h = _mamba_block(h, ssm_norm[i], ssm_w_in[i], ssm_conv_w[i], ssm_conv_b[i], ssm_dt_bias[i],
                             ssm_a_log[i], ssm_d[i], ssm_gate_norm[i], ssm_w_out[i])
        else:
            j = layer - n_a
            h = _attention_block(h, k_shared, v_shared, attn_norm[j], w_q[j], q_norm[j], sinks[j],
                                 rel_bias, w_o[j])
        h = _ffn_half_step(h.reshape(n, d), ffn_norm[layer, 1], w1, w3, w2, layer, 1).reshape(bsz, t, d)
    return h
```

```python
import functools
import math

import jax
import jax.numpy as jnp
from jax import lax
from jax.experimental import pallas as pl
from jax.experimental.pallas import tpu as pltpu

F32 = jnp.float32
BF16 = jnp.bfloat16

EPS = 1e-6
FFN_HALF = 0.5
LOG2E = 1.4426950408889634

SSM_HEAD_DIM = 64
SSM_GROUPS = 4
SSM_STATE = 128
SSM_CONV = 4
ATT_HEAD_DIM = 64
ATT_KV_HEADS = 2
ATT_WINDOW = 128
REL_BUCKETS = 32

V7X_LANES = 128
V7X_SUBLANES = 8
V7X_VMEM_LIMIT_BYTES = 56 * 1024 * 1024

FFN_TOKEN_TILE = 1024
FFN_CHUNK = 256
SSM_TOKEN_TILE = 256
SSM_SUBCHUNK = 128
SSD_MATMUL_LAG = 3
ATT_TOKEN_TILE = 512


def _rms_scale(x, g):
    ms = jnp.mean(x * x, axis=-1, keepdims=True)
    return x * lax.rsqrt(ms + EPS) * g


def _silu(x):
    hx = 0.5 * x
    return hx + hx * jnp.tanh(hx)


def _const_spec(shape):
    zeros = (0,) * len(shape)
    return pl.BlockSpec(shape, lambda *_: zeros, pipeline_mode=pl.Buffered(1))


def _pair_lane_mask():
    return lax.broadcasted_iota(jnp.int32, (1, 2 * ATT_HEAD_DIM), 1) < ATT_HEAD_DIM


def _pair_rms_scale(x, lane_lo, post):
    xsq = x * x
    s_lo = jnp.sum(jnp.where(lane_lo, xsq, 0.0), axis=-1, keepdims=True)
    s_hi = jnp.sum(jnp.where(lane_lo, 0.0, xsq), axis=-1, keepdims=True)
    inv = 1.0 / ATT_HEAD_DIM
    return jnp.where(lane_lo, lax.rsqrt(s_lo * inv + EPS) * post, lax.rsqrt(s_hi * inv + EPS) * post)


def _ffn_kernel(*refs, n_chunks, fc, with_kv):
    if with_kv:
        h_ref, g_ref, w1_ref, w3_ref, w2_ref, kvg_ref, wkv_ref, kn_ref, o_ref, k_ref, v_ref = refs
    else:
        h_ref, g_ref, w1_ref, w3_ref, w2_ref, o_ref = refs
    h = h_ref[...]
    xn = h * lax.rsqrt(jnp.mean(h * h, axis=-1, keepdims=True) + EPS)
    u = (xn * g_ref[...]).astype(BF16)
    if with_kv:
        kw = k_ref.shape[1]
        kv = jnp.dot((xn * kvg_ref[...]).astype(BF16), wkv_ref[...], preferred_element_type=F32)
        k = kv[:, :kw]
        k_ref[...] = (k * _pair_rms_scale(k, _pair_lane_mask(), 1.0) * kn_ref[...]).astype(k_ref.dtype)
        v_ref[...] = kv[:, kw:].astype(v_ref.dtype)
    for c in range(n_chunks):
        cols = slice(c * fc, (c + 1) * fc)
        a = jnp.dot(u, w1_ref[:, cols].astype(BF16), preferred_element_type=F32)
        b = jnp.dot(u, w3_ref[:, cols].astype(BF16), preferred_element_type=F32)
        gate = (_silu(a) * b).astype(BF16)
        d = jnp.dot(gate, w2_ref[cols, :].astype(BF16), preferred_element_type=F32)
        if c == 0:
            o_ref[...] = d
        else:
            o_ref[...] += d
    o_ref[...] = h + FFN_HALF * o_ref[...]


def _ffn_half_step(h, g, w1, w3, w2, layer, half, kv_params=None):
    n, d = h.shape
    f = w1.shape[-1]
    tm = min(FFN_TOKEN_TILE, n)
    fc = FFN_CHUNK
    assert n % tm == 0 and f % fc == 0
    with_kv = kv_params is not None

    def wsel(*_):
        return (layer, half, 0, 0)

    one = pl.Buffered(1)
    row = pl.BlockSpec((tm, d), lambda i: (i, 0))
    in_specs = [
        row,
        _const_spec((1, d)),
        pl.BlockSpec((None, None, d, f), wsel, pipeline_mode=one),
        pl.BlockSpec((None, None, d, f), wsel, pipeline_mode=one),
        pl.BlockSpec((None, None, f, d), wsel, pipeline_mode=one),
    ]
    args = [h, g.reshape(1, d), w1, w3, w2]
    out_shape = jax.ShapeDtypeStruct((n, d), F32)
    out_specs = row
    if with_kv:
        kv_norm, w_kv16, k_norm = kv_params
        kw = ATT_KV_HEADS * ATT_HEAD_DIM
        assert w_kv16.shape == (d, 2 * kw) and kw == V7X_LANES
        kn = jnp.tile(k_norm.astype(F32), ATT_KV_HEADS).reshape(1, kw)
        in_specs += [_const_spec((1, d)), _const_spec((d, 2 * kw)), _const_spec((1, kw))]
        args += [kv_norm.reshape(1, d), w_kv16, kn]
        kv_spec = pl.BlockSpec((tm, kw), lambda i: (i, 0))
        out_shape = (out_shape, jax.ShapeDtypeStruct((n, kw), BF16), jax.ShapeDtypeStruct((n, kw), BF16))
        out_specs = (row, kv_spec, kv_spec)
    return pl.pallas_call(
        functools.partial(_ffn_kernel, n_chunks=f // fc, fc=fc, with_kv=with_kv),
        out_shape=out_shape,
        grid=(n // tm,),
        in_specs=in_specs,
        out_specs=out_specs,
        compiler_params=pltpu.CompilerParams(
            dimension_semantics=("arbitrary",), vmem_limit_bytes=V7X_VMEM_LIMIT_BYTES),
        name="ffn_kv_half_step" if with_kv else "ffn_half_step",
    )(*args)


N_HANDOVER = 8


def _pair_cols(v, r0, lane_lo):
    if v.shape[0] == 1:
        return jnp.where(lane_lo, v[:, r0:r0 + 1], v[:, r0 + 1:r0 + 2])
    idx = jnp.where(jnp.broadcast_to(lane_lo, v.shape), r0, r0 + 1)
    return jnp.take_along_axis(v, idx, axis=1)


def _bf16_terms(x):
    hi = x.astype(BF16)
    r1 = x - hi.astype(F32)
    mid = r1.astype(BF16)
    lo = (r1 - mid.astype(F32)).astype(BF16)
    return hi, mid, lo


def _exact_f32_dot(a, b, split_rhs):
    if split_rhs:
        n = b.shape[1]
        out = jnp.dot(a, jnp.concatenate(_bf16_terms(b), axis=1), preferred_element_type=F32)
        return out[:, :n] + out[:, n:2 * n] + out[:, 2 * n:]
    m = a.shape[0]
    out = jnp.dot(jnp.concatenate(_bf16_terms(a), axis=0), b, preferred_element_type=F32)
    return out[:m] + out[m:2 * m] + out[2 * m:]


def _mamba_stage_a(h_ref, g_ref, win_ref, wdt_ref, wdtT_ref, cw_ref, cb_ref,
                   dtb_ref, dtbT_ref, alog_ref, alogT_ref, xpad_ref, bufs, *, n_heads):
    hs_ref, z_ref, xs_ref, b_ref, c_ref, dt_ref, da_ref, daT_ref = bufs
    L = h_ref.shape[0]
    d_inner = z_ref.shape[1]
    pw = 2 * SSM_HEAD_DIM
    n_slab = xpad_ref.shape[0]
    n_xs = xs_ref.shape[0]
    pad = V7X_SUBLANES
    zc = 4 * pw

    h = h_ref[...]
    hs_ref[...] = h
    u = _rms_scale(h, g_ref[...]).astype(BF16)
    yield

    def softplus(x):
        return jnp.maximum(x, 0.0) + jnp.log1p(jnp.exp(-jnp.abs(x)))

    x0 = d_inner
    dt_raw = jnp.dot(u, wdt_ref[...], preferred_element_type=F32)
    dt_rawT = lax.dot_general(wdtT_ref[...], u, (((1,), (1,)), ((), ())),
                              preferred_element_type=F32)
    dt = softplus(dt_raw + dtb_ref[...])
    dtT = softplus(dt_rawT + dtbT_ref[...])
    dt_ref[...] = dt
    da_ref[...] = dt * (-jnp.exp(alog_ref[...]))
    daT_ref[...] = dtT * (-jnp.exp(alogT_ref[...]))
    yield

    def x_unit(j0):
        xj = jnp.dot(u, win_ref[:, x0 + j0 * pw:x0 + (j0 + 2) * pw], preferred_element_type=F32)
        xpad_ref[j0, pad:pad + L, :] = xj[:, :pw]
        xpad_ref[j0 + 1, pad:pad + L, :] = xj[:, pw:]
        for j in (j0, j0 + 1):
            acc = cb_ref[:, j * pw:(j + 1) * pw]
            for k in range(SSM_CONV):
                off = pad - (SSM_CONV - 1) + k
                acc = acc + cw_ref[k:k + 1, j * pw:(j + 1) * pw] * xpad_ref[j, off:off + L, :]
            xpad_ref[j, 0:pad, :] = xpad_ref[j, L:L + pad, :]
            act = _silu(acc)
            if j < n_xs:
                xs_ref[j] = act
            elif j < n_xs + SSM_GROUPS:
                b_ref[j - n_xs] = act.astype(BF16)
            else:
                c_ref[j - n_xs - SSM_GROUPS] = act.astype(BF16)

    def z_unit(c0):
        z_ref[:, c0:c0 + zc] = jnp.dot(u, win_ref[:, c0:c0 + zc], preferred_element_type=F32)

    x_units = [functools.partial(x_unit, j0) for j0 in range(0, n_slab, 2)]
    z_units = [functools.partial(z_unit, c0) for c0 in range(0, d_inner, zc)]
    order = []
    per = len(x_units) // len(z_units)
    for i, zu in enumerate(z_units):
        order += x_units[i * per:(i + 1) * per] + [zu]
    order += x_units[len(z_units) * per:]
    for unit in order:
        unit()
        yield


def _mamba_stage_b(bufs, dskip_ref, gn_ref, wo_ref, o_ref, state_ref, y_ref, *, n_heads):
    hs_ref, z_ref, xs_ref, b_ref, c_ref, dt_ref, da_ref, daT_ref = bufs
    L = hs_ref.shape[0]
    ls = SSM_SUBCHUNK
    d_inner = z_ref.shape[1]
    hp = SSM_HEAD_DIM
    pw = 2 * hp
    pairs_per_group = n_heads // SSM_GROUPS // 2

    ri = lax.broadcasted_iota(jnp.int32, (ls, ls), 0)
    ci = lax.broadcasted_iota(jnp.int32, (ls, ls), 1)
    causal = ri >= ci
    tri = jnp.where(causal, 1.0, 0.0).astype(BF16)
    triT = jnp.where(ci >= ri, 1.0, 0.0).astype(BF16)
    lane_lo = lax.broadcasted_iota(jnp.int32, (1, pw), 1) < hp
    pending = []
    state_todo = []

    def update_state():
        while state_todo:
            g_, bg_, xw_, el_ = state_todo.pop(0)
            inc = lax.dot_general(bg_, xw_, (((0,), (0,)), ((), ())),
                                  preferred_element_type=F32)
            state_ref[g_] = state_ref[g_] * el_ + inc

    def flush(keep=0):
        while len(pending) > keep:
            lhs, rhs_, off, rows_, p_ = pending.pop(0)
            y_ref[rows_, p_ * pw:(p_ + 1) * pw] = jnp.dot(lhs, rhs_, preferred_element_type=F32) + off

    for sc in range(L // ls):
        rows = slice(sc * ls, (sc + 1) * ls)
        acum = _exact_f32_dot(tri, da_ref[rows, :], split_rhs=True)
        acumT = _exact_f32_dot(daT_ref[:, rows], triT, split_rhs=False)
        a_last = acum[ls - 1:ls, :]
        e_acum = jnp.exp(acum)
        e_last = jnp.exp(a_last)
        dt_s = dt_ref[rows, :]
        w_end = jnp.exp(a_last - acum) * dt_s
        acum2 = acum * LOG2E
        acumT2 = acumT * LOG2E
        update_state()
        cbs = []
        y_offs = []
        for g in range(SSM_GROUPS):
            cg = c_ref[g, rows, :]
            cbs.append(lax.dot_general(cg, b_ref[g, rows, :], (((1,), (1,)), ((), ())),
                                       preferred_element_type=F32).astype(BF16))
            y_offs.append(jnp.dot(cg, state_ref[g].astype(BF16), preferred_element_type=F32))
        yield
        for g in range(SSM_GROUPS):
            bg = b_ref[g, rows, :]
            cb16 = cbs[g]
            y_off = y_offs[g]
            xw_parts = []
            el_parts = []
            for j in range(pairs_per_group):
                p = g * pairs_per_group + j
                r0 = 2 * p
                xs_pair = xs_ref[p, rows, :]
                xd = xs_pair * _pair_cols(dt_s, r0, lane_lo)
                rhs = jnp.concatenate([jnp.where(lane_lo, xd, 0.0), jnp.where(lane_lo, 0.0, xd)],
                                      axis=0).astype(BF16)
                ms = []
                for r in (r0, r0 + 1):
                    seg = acum2[:, r:r + 1] - acumT2[r:r + 1, :]
                    dec = jnp.exp2(jnp.where(causal, seg, -jnp.inf))
                    ms.append(cb16 * dec.astype(BF16))
                flush(keep=SSD_MATMUL_LAG - 1)
                pending.append((jnp.concatenate(ms, axis=1), rhs,
                                y_off[:, j * pw:(j + 1) * pw] * _pair_cols(e_acum, r0, lane_lo), rows, p))
                xw_parts.append((xs_pair * _pair_cols(w_end, r0, lane_lo)).astype(BF16))
                el_parts.append(_pair_cols(e_last, r0, lane_lo))
                if j % 2 == 1 and j + 1 < pairs_per_group:
                    yield
            xw = jnp.concatenate(xw_parts, axis=1)
            el = jnp.concatenate(el_parts, axis=1)
            update_state()
            state_todo.append((g, bg, xw, el))
            yield

    flush()
    update_state()
    gw = d_inner // SSM_GROUPS
    slabs_per_group = gw // pw
    for g in range(SSM_GROUPS):
        cols = slice(g * gw, (g + 1) * gw)
        xs_g = jnp.concatenate([xs_ref[g * slabs_per_group + i] for i in range(slabs_per_group)], axis=1)
        yg = (y_ref[:, cols] + dskip_ref[:, cols] * xs_g) * _silu(z_ref[:, cols])
        yn = _rms_scale(yg, gn_ref[:, cols]).astype(BF16)
        part = jnp.dot(yn, wo_ref[cols, :], preferred_element_type=F32)
        if g == 0:
            o_ref[...] = hs_ref[...] + part
        else:
            o_ref[...] += part
        yield


def _interleave(*gens):
    live = list(gens)
    while live:
        for gen in list(live):
            try:
                next(gen)
            except StopIteration:
                live.remove(gen)


def _mamba_kernel(h_ref, g_ref, win_ref, wdt_ref, wdtT_ref, cw_ref, cb_ref,
                  dtb_ref, dtbT_ref, alog_ref, alogT_ref, dskip_ref, gn_ref, wo_ref,
                  o_ref, xpad_ref, state_ref, y_ref, *handover, n_heads, tiles_per_seq):
    slots = (handover[:N_HANDOVER], handover[N_HANDOVER:])
    s = pl.program_id(0)
    pad = V7X_SUBLANES

    @pl.when(s == 0)
    def _():
        for ref in slots[1]:
            ref[...] = jnp.zeros_like(ref)

    @pl.when(lax.rem(s, tiles_per_seq) == 0)
    def _():
        xpad_ref[:, 0:pad, :] = jnp.zeros((xpad_ref.shape[0], pad, xpad_ref.shape[2]), F32)

    @pl.when((s == 0) | (lax.rem(s - 1, tiles_per_seq) == 0))
    def _():
        state_ref[...] = jnp.zeros_like(state_ref)

    def step(write_slot):
        _interleave(
            _mamba_stage_b(slots[1 - write_slot], dskip_ref, gn_ref, wo_ref, o_ref, state_ref, y_ref,
                           n_heads=n_heads),
            _mamba_stage_a(h_ref, g_ref, win_ref, wdt_ref, wdtT_ref, cw_ref, cb_ref,
                           dtb_ref, dtbT_ref, alog_ref, alogT_ref, xpad_ref, slots[write_slot],
                           n_heads=n_heads))

    parity = lax.rem(s, 2)

    @pl.when(parity == 0)
    def _():
        step(0)

    @pl.when(parity == 1)
    def _():
        step(1)


def _mamba_block(h3, norm_g, w_in, conv_w, conv_b, dt_bias, a_log, d_skip, gate_norm, w_out):
    bsz, t, d = h3.shape
    n_heads = dt_bias.shape[0]
    d_inner = n_heads * SSM_HEAD_DIM
    conv_dim = d_inner + 2 * SSM_GROUPS * SSM_STATE
    pw = 2 * SSM_HEAD_DIM
    assert w_in.shape == (d, d_inner + conv_dim + n_heads)
    assert n_heads % (2 * SSM_GROUPS) == 0 and n_heads <= V7X_LANES
    assert pw == V7X_LANES and SSM_STATE == V7X_LANES and (conv_dim // pw) % 2 == 0
    L = SSM_TOKEN_TILE
    assert t % L == 0 and L % SSM_SUBCHUNK == 0
    hpg = n_heads // SSM_GROUPS
    tiles_per_seq = t // L
    n_tiles = bsz * tiles_per_seq

    w_in16 = w_in.astype(BF16)
    wdt = w_in16[:, d_inner + conv_dim:]
    wdtT = wdt.T
    lane_pad = V7X_LANES - n_heads
    wdt_pad = jnp.pad(wdt, ((0, 0), (0, lane_pad)))
    dskip_row = jnp.repeat(d_skip.astype(F32), SSM_HEAD_DIM).reshape(1, d_inner)

    arrays = [
        norm_g.reshape(1, d),
        w_in16, wdt_pad, wdtT,
        conv_w.astype(F32), conv_b.reshape(1, conv_dim).astype(F32),
        jnp.pad(dt_bias.reshape(1, n_heads).astype(F32), ((0, 0), (0, lane_pad))),
        dt_bias.reshape(n_heads, 1).astype(F32),
        jnp.pad(a_log.reshape(1, n_heads).astype(F32), ((0, 0), (0, lane_pad))),
        a_log.reshape(n_heads, 1).astype(F32),
        dskip_row,
        gate_norm.reshape(1, d_inner).astype(F32),
        w_out.astype(BF16),
    ]
    handover = [
        pltpu.VMEM((L, d), F32),
        pltpu.VMEM((L, d_inner), F32),
        pltpu.VMEM((d_inner // pw, L, pw), F32),
        pltpu.VMEM((SSM_GROUPS, L, SSM_STATE), BF16),
        pltpu.VMEM((SSM_GROUPS, L, SSM_STATE), BF16),
        pltpu.VMEM((L, V7X_LANES), F32),
        pltpu.VMEM((L, V7X_LANES), F32),
        pltpu.VMEM((n_heads, L), F32),
    ]
    assert len(handover) == N_HANDOVER
    in_specs = [pl.BlockSpec((L, d), lambda s: (jnp.minimum(s, n_tiles - 1), 0))]
    in_specs += [_const_spec(a.shape) for a in arrays]
    out = pl.pallas_call(
        functools.partial(_mamba_kernel, n_heads=n_heads, tiles_per_seq=tiles_per_seq),
        out_shape=jax.ShapeDtypeStruct((bsz * t, d), F32),
        grid=(n_tiles + 1,),
        in_specs=in_specs,
        out_specs=pl.BlockSpec((L, d), lambda s: (jnp.maximum(s - 1, 0), 0)),
        scratch_shapes=[
            pltpu.VMEM((conv_dim // pw, L + V7X_SUBLANES, pw), F32),
            pltpu.VMEM((SSM_GROUPS, SSM_STATE, hpg * SSM_HEAD_DIM), F32),
            pltpu.VMEM((L, d_inner), F32),
        ] + handover + handover,
        compiler_params=pltpu.CompilerParams(
            dimension_semantics=("arbitrary",),
            vmem_limit_bytes=V7X_VMEM_LIMIT_BYTES),
        name="mamba2_block",
    )(h3.reshape(bsz * t, d), *arrays)
    return out.reshape(bsz, t, d)


def _t5_bucket_table(blk):
    qi = jnp.arange(blk)[:, None] + blk
    kj = jnp.arange(2 * blk)[None, :]
    dist = qi - kj
    n = jnp.maximum(dist, 0)
    max_exact = REL_BUCKETS // 2
    nf = jnp.maximum(n, 1).astype(F32)
    large = max_exact + (jnp.log(nf / max_exact) / math.log(ATT_WINDOW / max_exact)
                         * (REL_BUCKETS - max_exact)).astype(jnp.int32)
    large = jnp.minimum(large, REL_BUCKETS - 1)
    bucket = jnp.where(n < max_exact, n, large)
    in_window = (dist >= 0) & (dist < ATT_WINDOW)
    return jnp.where(in_window, bucket, REL_BUCKETS).astype(jnp.int32)


def _attn_kernel(sink_ref, h_ref, g_ref, wq_ref, qn_ref, kp_ref, kc_ref, vp_ref, vc_ref,
                 bucket_ref, relb_ref, wo_ref, o_ref, bias_ref, *, n_heads):
    tq = h_ref.shape[0]
    blk = ATT_WINDOW
    hd = ATT_HEAD_DIM
    pw = 2 * hd
    group = n_heads // ATT_KV_HEADS
    pairs = group // 2
    first = (pl.program_id(0) == 0) & (pl.program_id(1) == 0)

    @pl.when(first)
    def _():
        bucket = bucket_ref[...]
        for hh in range(n_heads):
            row = jnp.broadcast_to(relb_ref[hh:hh + 1, :] * LOG2E, (blk, V7X_LANES))
            for c0 in range(0, 2 * blk, V7X_LANES):
                bias_ref[hh, :, c0:c0 + V7X_LANES] = jnp.take_along_axis(
                    row, bucket[:, c0:c0 + V7X_LANES], axis=1)

    h = h_ref[...]
    u = _rms_scale(h, g_ref[...]).astype(BF16)
    q = jnp.dot(u, wq_ref[...].astype(BF16), preferred_element_type=F32)
    lane_lo = _pair_lane_mask()

    q16 = jnp.concatenate(
        [(q[:, p * pw:(p + 1) * pw]
          * _pair_rms_scale(q[:, p * pw:(p + 1) * pw], lane_lo, hd ** -0.5 * LOG2E)).astype(BF16)
         for p in range(n_heads // 2)], axis=1)

    kall = jnp.concatenate([kp_ref[...], kc_ref[...]], axis=0).astype(F32) * qn_ref[...]
    vall = jnp.concatenate([vp_ref[...], vc_ref[...]], axis=0).astype(F32)
    kswap = pltpu.roll(kall, hd, 1)
    vswap = pltpu.roll(vall, hd, 1)
    k_slots = []
    v_slots = []
    for kvh in range(ATT_KV_HEADS):
        src_k, alt_k = (kall, kswap) if kvh == 0 else (kswap, kall)
        src_v, alt_v = (vall, vswap) if kvh == 0 else (vswap, vall)
        k_slots.append((jnp.where(lane_lo, src_k, 0.0).astype(BF16), jnp.where(lane_lo, 0.0, alt_k).astype(BF16)))
        v_slots.append((jnp.where(lane_lo, src_v, 0.0).astype(BF16), jnp.where(lane_lo, 0.0, alt_v).astype(BF16)))

    kj = lax.broadcasted_iota(jnp.int32, (1, 2 * blk), 1)
    key_ok = kj >= jnp.where(pl.program_id(1) > 0, 0, blk)

    outs = {}
    for kvh in range(ATT_KV_HEADS):
        for r in range(tq // blk):
            qrows = slice(r * blk, (r + 1) * blk)
            krows = slice(r * blk, r * blk + 2 * blk)
            out_pairs = []
            q_stack = jnp.concatenate(
                [q16[qrows, (kvh * pairs + j) * pw:(kvh * pairs + j + 1) * pw] for j in range(pairs)],
                axis=0)
            s_slots = [lax.dot_general(q_stack, ks[krows, :], (((1,), (1,)), ((), ())),
                                       preferred_element_type=F32) for ks in k_slots[kvh]]
            v_stack = jnp.concatenate([vs[krows, :] for vs in v_slots[kvh]], axis=0)
            p_rows = []
            scales = []
            for j in range(pairs):
                probs = []
                rinv = []
                for e in range(2):
                    hh = kvh * group + 2 * j + e
                    s = s_slots[e][j * blk:(j + 1) * blk, :] + bias_ref[hh]
                    if r == 0:
                        s = jnp.where(key_ok, s, -jnp.inf)
                    sink = sink_ref[hh] * LOG2E
                    m = jnp.maximum(jnp.max(s, axis=-1, keepdims=True), sink)
                    pexp = jnp.exp2(s - m)
                    denom = jnp.sum(pexp, axis=-1, keepdims=True) + jnp.exp2(sink - m)
                    probs.append(pexp.astype(BF16))
                    rinv.append(1.0 / denom)
                p_rows.append(jnp.concatenate(probs, axis=1))
                scales.append(jnp.where(lane_lo, rinv[0], rinv[1]))
            pv = jnp.dot(jnp.concatenate(p_rows, axis=0), v_stack, preferred_element_type=F32)
            for j in range(pairs):
                out_pairs.append((pv[j * blk:(j + 1) * blk, :] * scales[j]).astype(BF16))
            outs[(r, kvh)] = jnp.concatenate(out_pairs, axis=1)
    o_rows = [jnp.concatenate([outs[(r, k)] for k in range(ATT_KV_HEADS)], axis=1) for r in range(tq // blk)]
    o = jnp.concatenate(o_rows, axis=0)
    o_ref[...] = h + jnp.dot(o, wo_ref[...].astype(BF16), preferred_element_type=F32)


def _attention_block(h3, k, v, norm_g, w_q, q_norm, sinks, rel_bias, w_o):
    bsz, t, d = h3.shape
    n_heads = sinks.shape[0]
    hd = ATT_HEAD_DIM
    kw = ATT_KV_HEADS * hd
    blk = ATT_WINDOW
    assert w_q.shape == (d, n_heads * hd) and w_o.shape == (n_heads * hd, d)
    assert n_heads % (2 * ATT_KV_HEADS) == 0 and kw == V7X_LANES
    tq = min(ATT_TOKEN_TILE, t)
    assert t % tq == 0 and tq % blk == 0
    ratio = tq // blk
    bucket = _t5_bucket_table(blk)
    qn = jnp.tile(q_norm.astype(F32), ATT_KV_HEADS).reshape(1, kw)
    assert rel_bias.shape == (REL_BUCKETS, n_heads) and REL_BUCKETS < V7X_LANES
    relb_rows = jnp.pad(rel_bias.T.astype(F32), ((0, 0), (0, V7X_LANES - REL_BUCKETS)), constant_values=-jnp.inf)
    k3 = k.reshape(bsz, t, kw)
    v3 = v.reshape(bsz, t, kw)

    def prev_blk(b, i, *_):
        return (b, jnp.maximum(i * ratio - 1, 0), 0)

    def cur_blk(b, i, *_):
        return (b, i, 0)

    def const2(b, i, *_):
        return (0, 0)

    one = pl.Buffered(1)
    grid_spec = pltpu.PrefetchScalarGridSpec(
        num_scalar_prefetch=1,
        grid=(bsz, t // tq),
        in_specs=[
            pl.BlockSpec((None, tq, d), cur_blk),
            pl.BlockSpec((1, d), const2, pipeline_mode=one),
            pl.BlockSpec((d, n_heads * hd), const2, pipeline_mode=one),
            pl.BlockSpec((1, kw), const2, pipeline_mode=one),
            pl.BlockSpec((None, blk, kw), prev_blk),
            pl.BlockSpec((None, tq, kw), cur_blk),
            pl.BlockSpec((None, blk, kw), prev_blk),
            pl.BlockSpec((None, tq, kw), cur_blk),
            pl.BlockSpec((blk, 2 * blk), const2, pipeline_mode=one),
            pl.BlockSpec((n_heads, V7X_LANES), const2, pipeline_mode=one),
            pl.BlockSpec((n_heads * hd, d), const2, pipeline_mode=one),
        ],
        out_specs=pl.BlockSpec((None, tq, d), cur_blk),
        scratch_shapes=[pltpu.VMEM((n_heads, blk, 2 * blk), F32)],
    )
    return pl.pallas_call(
        functools.partial(_attn_kernel, n_heads=n_heads),
        out_shape=jax.ShapeDtypeStruct((bsz, t, d), F32),
        grid_spec=grid_spec,
        compiler_params=pltpu.CompilerParams(
            dimension_semantics=("arbitrary", "arbitrary"),
            vmem_limit_bytes=V7X_VMEM_LIMIT_BYTES),
        name="swa_block",
    )(sinks.astype(F32), h3, norm_g.reshape(1, d), w_q.astype(F32), qn,
      k3, k3, v3, v3, bucket, relb_rows, w_o.astype(F32))


def kernel(x, ffn_norm, ffn_w1, ffn_w3, ffn_w2, ssm_norm, ssm_w_in, ssm_conv_w, ssm_conv_b, ssm_dt_bias, ssm_a_log, ssm_d, ssm_gate_norm, ssm_w_out, kv_norm, w_kv, k_norm, attn_norm, w_q, q_norm, sinks, w_o, rel_bias):
    bsz, t, d = x.shape
    depth = ffn_norm.shape[0]
    n_a = ssm_norm.shape[0]
    n = bsz * t
    w1 = ffn_w1.astype(F32)
    w3 = ffn_w3.astype(F32)
    w2 = ffn_w2.astype(F32)

    h = x
    k_shared = v_shared = None
    for layer in range(depth):
        if layer == n_a and layer > 0:
            h, k_shared, v_shared = _ffn_half_step(h.reshape(n, d), ffn_norm[layer, 0], w1, w3, w2, layer, 0,
                                                   kv_params=(kv_norm, w_kv.astype(BF16), k_norm))
        else:
            h = _ffn_half_step(h.reshape(n, d), ffn_norm[layer, 0], w1, w3, w2, layer, 0)
        h = h.reshape(bsz, t, d)
        if layer < n_a:
            i = layer
            h = _mamba_block(h, ssm_norm[i], ssm_w_in[i], ssm_conv_w[i], ssm_conv_b[i], ssm_dt_bias[i],
                             ssm_a_log[i], ssm_d[i], ssm_gate_norm[i], ssm_w_out[i])
        else:
            j = layer - n_a
            h = _attention_block(h, k_shared, v_shared, attn_norm[j], w_q[j], q_norm[j], sinks[j],
                                 rel_bias, w_o[j])
        h = _ffn_half_step(h.reshape(n, d), ffn_norm[layer, 1], w1, w3, w2, layer, 1).reshape(bsz, t, d)
    return h
```
